```python
import math
import jax, jax.numpy as jnp
from jax import lax
import numpy as np

D_MODEL = 2048
BATCH = 4
SEQ = 4096
DEPTH = 1

CHUNK = 64
NORM_EPS = 1e-6
N_BRANCH = 2
GDN_QK_HEADS = 4
GDN_V_HEADS = 8
GDN_HEAD_K = 128
GDN_HEAD_V = 128
GDN_KEY_DIM = GDN_QK_HEADS * GDN_HEAD_K
GDN_VAL_DIM = GDN_V_HEADS * GDN_HEAD_V
GDN_CONV = 4
RWKV_HEAD = 64
RWKV_HEADS = 16
RWKV_DIM = RWKV_HEADS * RWKV_HEAD
RWKV_DECAY_RANK = 96
RWKV_AAA_RANK = 96
RWKV_GATE_RANK = 256
RWKV_LN_EPS = 64e-5
N_EXPERTS = 32
TOP_K = 4
D_FF_EXPERT = D_MODEL
SWIGLU_ALPHA = 1.702
SWIGLU_LIMIT = 7.0
EXPERT_BLOCK = 256
COL_K = GDN_KEY_DIM
COL_V = 2 * GDN_KEY_DIM
COL_Z = COL_V + GDN_VAL_DIM
COL_B = COL_Z + GDN_VAL_DIM
COL_A = COL_B + GDN_V_HEADS
COL_GATE = COL_A + GDN_V_HEADS
COL_R = COL_GATE + N_BRANCH * D_MODEL
COL_RK = COL_R + RWKV_DIM
COL_RV = COL_RK + RWKV_DIM
N_IN = COL_RV + RWKV_DIM

kernel_name = "chunk_causal_gdn_rwkv7_gated_merge_moe_block"


def rms_norm(x, w, eps=NORM_EPS):
    xf = x.astype(jnp.float32)
    y = xf * lax.rsqrt(jnp.mean(xf * xf, axis=-1, keepdims=True) + eps)
    return (y * w.astype(jnp.float32)).astype(x.dtype)


def l2_normalize(x, eps=1e-6):
    xf = x.astype(jnp.float32)
    return xf * lax.rsqrt(jnp.sum(xf * xf, axis=-1, keepdims=True) + eps)


def token_shift(x):
    return jnp.pad(x, ((0, 0), (1, 0), (0, 0)))[:, :-1]


def causal_depthwise_conv(x, w):
    K, C = w.shape
    xp = jnp.pad(x, ((0, 0), (K - 1, 0), (0, 0)))
    return lax.conv_general_dilated(xp, w[:, None, :].astype(x.dtype), window_strides=(1,), padding='VALID',
                                    dimension_numbers=('NWC', 'WIO', 'NWC'), feature_group_count=C)


def gated_delta_rule(q, k, v, g, beta):
    B, S, H, dk = q.shape
    dv = v.shape[-1]
    n = S // CHUNK

    def chunks(t):
        return t.astype(jnp.float32).reshape(B, n, CHUNK, H, -1).transpose(1, 0, 3, 2, 4)

    q, k, v = chunks(q), chunks(k), chunks(v)
    g = jnp.cumsum(chunks(g[..., None])[..., 0], axis=-1)
    beta = chunks(beta[..., None])[..., 0]
    incl = jnp.tril(jnp.ones((CHUNK, CHUNK), dtype=bool))
    strict = jnp.tril(jnp.ones((CHUNK, CHUNK), dtype=bool), k=-1)
    diff = g[..., :, None] - g[..., None, :]
    decay = jnp.where(incl, jnp.exp(jnp.where(incl, diff, 0.0)), 0.0)
    kb = k * beta[..., None]
    lower = jnp.where(strict, jnp.einsum('nbhid,nbhjd->nbhij', kb, k) * decay, 0.0)
    rhs = jnp.concatenate([v * beta[..., None], kb * jnp.exp(g)[..., None]], axis=-1)
    sol = lax.linalg.triangular_solve(lower, rhs, left_side=True, lower=True, unit_diagonal=True)
    u, w = sol[..., :dv], sol[..., dv:]
    intra = jnp.where(incl, jnp.einsum('nbhid,nbhjd->nbhij', q, k) * decay, 0.0)

    def step(state, xs):
        q_c, k_c, u_c, w_c, g_c, a_c = xs
        v_new = u_c - jnp.einsum('bhck,bhkv->bhcv', w_c, state)
        o = (jnp.einsum('bhck,bhkv->bhcv', q_c * jnp.exp(g_c)[..., None], state)
             + jnp.einsum('bhij,bhjv->bhiv', a_c, v_new))
        g_last = g_c[..., -1:]
        k_dec = k_c * jnp.exp(g_last - g_c)[..., None]
        state = state * jnp.exp(g_last)[..., None] + jnp.einsum('bhck,bhcv->bhkv', k_dec, v_new)
        return state, o

    state0 = jnp.zeros((B, H, dk, dv), jnp.float32)
    _, o = lax.scan(step, state0, (q, k, u, w, g, intra))
    return o.transpose(1, 0, 3, 2, 4).reshape(B, S, H, dv)


def rwkv7_recurrence(r, w, k, v, a, b):
    B, S, H, N = r.shape
    xs = tuple(t.astype(jnp.float32).transpose(1, 0, 2, 3) for t in (r, w, k, v, a, b))

    def step(state, inp):
        r_t, w_t, k_t, v_t, a_t, b_t = inp
        sa = jnp.einsum('bhvk,bhk->bhv', state, a_t)
        state = (state * w_t[:, :, None, :] + sa[..., None] * b_t[:, :, None, :]
                 + v_t[..., None] * k_t[:, :, None, :])
        return state, jnp.einsum('bhvk,bhk->bhv', state, r_t)

    _, y = lax.scan(step, jnp.zeros((B, H, N, N), jnp.float32), xs)
    return y.transpose(1, 0, 2, 3)


def hybrid_mixer(h, w_in, gdn_conv, gdn_A_log, gdn_dt_bias, gdn_norm, rwkv_mix, rwkv_w0, rwkv_w1, rwkv_w2,
                 rwkv_a0, rwkv_a1, rwkv_a2, rwkv_g1, rwkv_g2, rwkv_k_k, rwkv_k_a, rwkv_r_k, rwkv_ln_w,
                 rwkv_ln_b, w_up_gdn, w_up_rwkv, w_out):
    B, S, D = h.shape
    f32 = jnp.float32
    ph = h @ w_in[:, :COL_R]
    dx = token_shift(h) - h
    xr, xw, xk, xv, xa, xg = [h + dx * rwkv_mix[i] for i in range(6)]
    r = xr @ w_in[:, COL_R:COL_RK]
    k_r = xk @ w_in[:, COL_RK:COL_RV]
    v_r = xv @ w_in[:, COL_RV:N_IN]

    qkv = jax.nn.silu(causal_depthwise_conv(ph[..., :COL_Z], gdn_conv))
    q = qkv[..., :COL_K].reshape(B, S, GDN_QK_HEADS, GDN_HEAD_K)
    k = qkv[..., COL_K:COL_V].reshape(B, S, GDN_QK_HEADS, GDN_HEAD_K)
    v = qkv[..., COL_V:COL_Z].reshape(B, S, GDN_V_HEADS, GDN_HEAD_V)
    q = l2_normalize(q) * (GDN_HEAD_K ** -0.5)
    k = l2_normalize(k)
    rep = GDN_V_HEADS // GDN_QK_HEADS
    q = jnp.repeat(q, rep, axis=2)
    k = jnp.repeat(k, rep, axis=2)
    z = ph[..., COL_Z:COL_B].reshape(B, S, GDN_V_HEADS, GDN_HEAD_V)
    beta = jax.nn.sigmoid(ph[..., COL_B:COL_A].astype(f32))
    g = -jnp.exp(gdn_A_log.astype(f32)) * jax.nn.softplus(ph[..., COL_A:COL_GATE].astype(f32)
                                                          + gdn_dt_bias.astype(f32))
    o = gated_delta_rule(q, k, v, g, beta)
    o = rms_norm(o, gdn_norm) * jax.nn.silu(z.astype(f32))
    y_gdn = o.reshape(B, S, GDN_VAL_DIM).astype(h.dtype)

    w_log = -jax.nn.softplus(-(rwkv_w0 + jnp.tanh(xw @ rwkv_w1) @ rwkv_w2).astype(f32)) - 0.5
    w_dec = jnp.exp(-jnp.exp(w_log))
    a = jax.nn.sigmoid((rwkv_a0 + (xa @ rwkv_a1) @ rwkv_a2).astype(f32))
    gate = jax.nn.sigmoid(xg @ rwkv_g1) @ rwkv_g2
    hs = lambda t: t.reshape(B, S, RWKV_HEADS, RWKV_HEAD)
    kk = l2_normalize(hs(k_r * rwkv_k_k))
    k_mod = k_r.astype(f32) * (1.0 + (a - 1.0) * rwkv_k_a.astype(f32))
    a_h = hs(a)
    y = rwkv7_recurrence(hs(r), hs(w_dec), hs(k_mod), hs(v_r), -kk, kk * a_h)
    mu = jnp.mean(y, axis=-1, keepdims=True)
    var = jnp.mean(jnp.square(y - mu), axis=-1, keepdims=True)
    y = (y - mu) * lax.rsqrt(var + RWKV_LN_EPS)
    y = y.reshape(B, S, RWKV_DIM) * rwkv_ln_w.astype(f32) + rwkv_ln_b.astype(f32)
    bonus = jnp.sum(hs(r).astype(f32) * hs(k_mod) * rwkv_r_k.astype(f32), axis=-1, keepdims=True) * hs(v_r).astype(f32)
    y = y + bonus.reshape(B, S, RWKV_DIM)
    y_rwkv = (y * gate.astype(f32)).astype(h.dtype)

    sg = jax.nn.sigmoid(ph[..., COL_GATE:COL_R].astype(f32)).astype(h.dtype).reshape(B, S, N_BRANCH, D)
    merged = sg[:, :, 0] * (y_gdn @ w_up_gdn) + sg[:, :, 1] * (y_rwkv @ w_up_rwkv)
    return merged @ w_out


def moe_ffn(h, router_w, router_b, w_gu, b_gu, w_down, b_down):
    B, S, D = h.shape
    T = B * S
    A = T * TOP_K
    F = D_FF_EXPERT
    ht = h.reshape(T, D)
    logits = (ht @ router_w + router_b).astype(jnp.float32)
    top_val, top_idx = lax.top_k(logits, TOP_K)
    probs = jax.nn.softmax(top_val, axis=-1)
    e_flat = top_idx.reshape(A).astype(jnp.int32)
    tok_flat = (jnp.arange(A, dtype=jnp.int32) // TOP_K)
    w_flat = probs.reshape(A)
    order = jnp.argsort(e_flat)
    e_sorted = e_flat[order]
    counts = jnp.bincount(e_flat, length=N_EXPERTS).astype(jnp.int32)
    start = jnp.cumsum(counts) - counts
    padded = (counts + EXPERT_BLOCK - 1) // EXPERT_BLOCK * EXPERT_BLOCK
    pad_end = jnp.cumsum(padded)
    pad_start = pad_end - padded
    pos = pad_start[e_sorted] + jnp.arange(A, dtype=jnp.int32) - start[e_sorted]
    P = A + N_EXPERTS * EXPERT_BLOCK
    NB = P // EXPERT_BLOCK
    buf_tok = jnp.full((P,), T, jnp.int32).at[pos].set(tok_flat[order])
    buf_w = jnp.zeros((P,), jnp.float32).at[pos].set(w_flat[order])
    block_start = jnp.arange(NB, dtype=jnp.int32) * EXPERT_BLOCK
    block_e = jnp.minimum(jnp.sum(pad_end[None, :] <= block_start[:, None], axis=1), N_EXPERTS - 1).astype(jnp.int32)
    h_pad = jnp.concatenate([ht, jnp.zeros((1, D), ht.dtype)], axis=0)

    def expert_block(args):
        tok, e = args
        xb = h_pad[tok]
        gu = xb @ w_gu[e] + b_gu[e]
        gl = jnp.minimum(gu[:, :F], SWIGLU_LIMIT)
        up = jnp.clip(gu[:, F:], -SWIGLU_LIMIT, SWIGLU_LIMIT)
        act = gl * jax.nn.sigmoid(SWIGLU_ALPHA * gl) * (up + 1.0)
        return act @ w_down[e] + b_down[e]

    yb = lax.map(expert_block, (buf_tok.reshape(NB, EXPERT_BLOCK), block_e))
    yb = yb.reshape(P, D) * buf_w[:, None].astype(h.dtype)
    y = jax.ops.segment_sum(yb, buf_tok, num_segments=T + 1)[:T]
    return y.reshape(B, S, D)


def setup_inputs(seed: int = 0) -> dict:
    key = jax.random.key(seed)
    ks = iter(list(jax.random.split(key, 48)))
    L, D, F, E = DEPTH, D_MODEL, D_FF_EXPERT, N_EXPERTS

    def nrm(shape, scale):
        return jax.random.normal(next(ks), shape, jnp.float32) * scale

    def gain(shape):
        return 1.0 + nrm(shape, 0.05)

    def unif(shape, lo, hi):
        return jax.random.uniform(next(ks), shape, jnp.float32, minval=lo, maxval=hi)

    x = nrm((BATCH, SEQ, D), 1.0)
    c = nrm((BATCH, D), 1.0)
    ada_w = nrm((L, D, 6 * D), D ** -0.5)
    ada_b = nrm((L, 6 * D), 0.02)
    norm_mix_pre = gain((L, D))
    norm_mix_post = gain((L, D))
    norm_ffn_pre = gain((L, D))
    norm_ffn_post = gain((L, D))
    w_in = nrm((L, D, N_IN), D ** -0.5)
    gdn_conv = nrm((L, GDN_CONV, COL_Z), GDN_CONV ** -0.5)
    gdn_A_log = jnp.log(unif((L, GDN_V_HEADS), 1.0, 16.0))
    dt = jnp.exp(unif((L, GDN_V_HEADS), math.log(1e-3), math.log(1e-1)))
    gdn_dt_bias = dt + jnp.log(-jnp.expm1(-dt))
    gdn_norm = gain((L, GDN_HEAD_V))
    rwkv_mix = unif((L, 6, D), 0.0, 1.0)
    rwkv_w0 = unif((L, RWKV_DIM), -6.0, 0.0)
    rwkv_w1 = nrm((L, D, RWKV_DECAY_RANK), D ** -0.5)
    rwkv_w2 = nrm((L, RWKV_DECAY_RANK, RWKV_DIM), 0.5 * RWKV_DECAY_RANK ** -0.5)
    rwkv_a0 = nrm((L, RWKV_DIM), 0.1)
    rwkv_a1 = nrm((L, D, RWKV_AAA_RANK), D ** -0.5)
    rwkv_a2 = nrm((L, RWKV_AAA_RANK, RWKV_DIM), RWKV_AAA_RANK ** -0.5)
    rwkv_g1 = nrm((L, D, RWKV_GATE_RANK), D ** -0.5)
    rwkv_g2 = nrm((L, RWKV_GATE_RANK, RWKV_DIM), RWKV_GATE_RANK ** -0.5)
    rwkv_k_k = 0.85 + nrm((L, RWKV_DIM), 0.05)
    rwkv_k_a = gain((L, RWKV_DIM))
    rwkv_r_k = nrm((L, RWKV_HEADS, RWKV_HEAD), 0.1)
    rwkv_ln_w = gain((L, RWKV_DIM))
    rwkv_ln_b = nrm((L, RWKV_DIM), 0.02)
    w_up_gdn = nrm((L, GDN_VAL_DIM, D), GDN_VAL_DIM ** -0.5)
    w_up_rwkv = nrm((L, RWKV_DIM, D), RWKV_DIM ** -0.5)
    w_out = nrm((L, D, D), D ** -0.5)
    router_w = nrm((L, D, E), D ** -0.5)
    router_b = nrm((L, E), 0.01)
    exp_w_gu = nrm((L, E, D, 2 * F), D ** -0.5)
    exp_b_gu = nrm((L, E, 2 * F), 0.02)
    exp_w_down = nrm((L, E, F, D), F ** -0.5)
    exp_b_down = nrm((L, E, D), 0.02)
    return {"x": x, "c": c, "ada_w": ada_w, "ada_b": ada_b,
            "norm_mix_pre": norm_mix_pre, "norm_mix_post": norm_mix_post,
            "norm_ffn_pre": norm_ffn_pre, "norm_ffn_post": norm_ffn_post,
            "w_in": w_in, "gdn_conv": gdn_conv, "gdn_A_log": gdn_A_log, "gdn_dt_bias": gdn_dt_bias,
            "gdn_norm": gdn_norm, "rwkv_mix": rwkv_mix, "rwkv_w0": rwkv_w0, "rwkv_w1": rwkv_w1,
            "rwkv_w2": rwkv_w2, "rwkv_a0": rwkv_a0, "rwkv_a1": rwkv_a1, "rwkv_a2": rwkv_a2,
            "rwkv_g1": rwkv_g1, "rwkv_g2": rwkv_g2, "rwkv_k_k": rwkv_k_k, "rwkv_k_a": rwkv_k_a,
            "rwkv_r_k": rwkv_r_k, "rwkv_ln_w": rwkv_ln_w, "rwkv_ln_b": rwkv_ln_b,
            "w_up_gdn": w_up_gdn, "w_up_rwkv": w_up_rwkv, "w_out": w_out,
            "router_w": router_w, "router_b": router_b, "exp_w_gu": exp_w_gu, "exp_b_gu": exp_b_gu,
            "exp_w_down": exp_w_down, "exp_b_down": exp_b_down}


def reference(x, c, ada_w, ada_b, norm_mix_pre, norm_mix_post, norm_ffn_pre, norm_ffn_post, w_in, gdn_conv,
              gdn_A_log, gdn_dt_bias, gdn_norm, rwkv_mix, rwkv_w0, rwkv_w1, rwkv_w2, rwkv_a0, rwkv_a1, rwkv_a2,
              rwkv_g1, rwkv_g2, rwkv_k_k, rwkv_k_a, rwkv_r_k, rwkv_ln_w, rwkv_ln_b, w_up_gdn, w_up_rwkv, w_out,
              router_w, router_b, exp_w_gu, exp_b_gu, exp_w_down, exp_b_down):
    for l in range(DEPTH):
        mod = jax.nn.silu(c) @ ada_w[l] + ada_b[l]
        sh1, sc1, gt1, sh2, sc2, gt2 = [m[:, None, :] for m in jnp.split(mod, 6, axis=-1)]
        h = rms_norm(x, norm_mix_pre[l]) * (1.0 + sc1) + sh1
        y = hybrid_mixer(h, w_in[l], gdn_conv[l], gdn_A_log[l], gdn_dt_bias[l], gdn_norm[l], rwkv_mix[l],
                         rwkv_w0[l], rwkv_w1[l], rwkv_w2[l], rwkv_a0[l], rwkv_a1[l], rwkv_a2[l], rwkv_g1[l],
                         rwkv_g2[l], rwkv_k_k[l], rwkv_k_a[l], rwkv_r_k[l], rwkv_ln_w[l], rwkv_ln_b[l],
                         w_up_gdn[l], w_up_rwkv[l], w_out[l])
        x = x + gt1 * rms_norm(y, norm_mix_post[l])
        h = rms_norm(x, norm_ffn_pre[l]) * (1.0 + sc2) + sh2
        y = moe_ffn(h, router_w[l], router_b[l], exp_w_gu[l], exp_b_gu[l], exp_w_down[l], exp_b_down[l])
        x = x + gt2 * rms_norm(y, norm_ffn_post[l])
    return x
```

```python
import functools

import jax
import jax.numpy as jnp
from jax import lax
from jax.experimental import pallas as pl
from jax.experimental.pallas import tpu as pltpu

f32 = jnp.float32
bf16 = jnp.bfloat16
i32 = jnp.int32

NORM_EPS = 1e-6
CHUNK = 64
GDN_QK_HEADS = 4
GDN_V_HEADS = 8
GDN_HEAD = 128
GDN_KEY_DIM = GDN_QK_HEADS * GDN_HEAD
GDN_VAL_DIM = GDN_V_HEADS * GDN_HEAD
GDN_CONV = 4
RWKV_HEAD = 64
RWKV_LN_EPS = 64e-5
RWKV_GROUP = 4
TOP_K = 4
SWIGLU_ALPHA = 1.702
SWIGLU_LIMIT = 7.0
COL_Z = 2 * GDN_KEY_DIM + GDN_VAL_DIM
COL_B = COL_Z + GDN_VAL_DIM

LANES = 128
SUBLANES = 8
VMEM_LIMIT_BYTES = 60 * 1024 * 1024

EXPERT_ROWS = 512
EXPERT_FF_TILE = 512
INV_BLOCK = 16


def _cparams(sem):
    return pltpu.CompilerParams(dimension_semantics=sem, vmem_limit_bytes=VMEM_LIMIT_BYTES)


def _dot(a, b):
    return jnp.dot(a.astype(bf16), b.astype(bf16), preferred_element_type=f32)


def _dot_nt(a, b):
    return lax.dot_general(a.astype(bf16), b.astype(bf16), (((1,), (1,)), ((), ())), preferred_element_type=f32)


def _dot_tn(a, b):
    return lax.dot_general(a.astype(bf16), b.astype(bf16), (((0,), (0,)), ((), ())), preferred_element_type=f32)


def _dot_f32(a, b):
    return jnp.dot(a, b, preferred_element_type=f32, precision=lax.Precision.HIGHEST)


def _softplus(z):
    return jnp.maximum(z, 0.0) + jnp.log1p(jnp.exp(-jnp.abs(z)))


def _iota(shape, axis):
    return lax.broadcasted_iota(i32, shape, axis)


def _ada_kernel(c_ref, w_ref, b_ref, o_ref):
    c = c_ref[...]
    s = c * jax.nn.sigmoid(c)
    o_ref[...] = _dot(s, w_ref[...]) + b_ref[...]


def _ada(c, w, b):
    B, D = c.shape
    N = w.shape[1]
    Bp = -(-B // SUBLANES) * SUBLANES
    cp = jnp.pad(c, ((0, Bp - B), (0, 0)))
    tn = 1024
    out = pl.pallas_call(
        _ada_kernel,
        grid=(N // tn,),
        in_specs=[pl.BlockSpec((Bp, D), lambda j: (0, 0)),
                  pl.BlockSpec((D, tn), lambda j: (0, j)),
                  pl.BlockSpec((1, tn), lambda j: (0, j))],
        out_specs=pl.BlockSpec((Bp, tn), lambda j: (0, j)),
        out_shape=jax.ShapeDtypeStruct((Bp, N), f32),
        compiler_params=_cparams(("arbitrary",)),
        name="ada",
    )(cp, w, b.reshape(1, N))
    return out[:B]


def _premix_kernel(x_ref, xp_ref, sc_ref, sh_ref, nw_ref, mix_ref, w1_ref, w2_ref, w0_ref, a1_ref, a2_ref, a0_ref,
                   g1_ref, g2_ref, h_ref, xr_ref, xk_ref, xv_ref, lw_ref, a_ref, gate_ref):
    i = pl.program_id(1)
    scale = nw_ref[...] * (1.0 + sc_ref[...])
    shift = sh_ref[...]

    def modulated_norm(x):
        ms = jnp.mean(x * x, axis=-1, keepdims=True)
        return x * lax.rsqrt(ms + NORM_EPS) * scale + shift

    h = modulated_norm(x_ref[...])
    hp = modulated_norm(xp_ref[...])[SUBLANES - 1:SUBLANES, :]
    hp = jnp.where(i == 0, 0.0, hp)
    rows = _iota(h.shape, 0)
    hprev = jnp.where(rows == 0, hp, pltpu.roll(h, 1, 0))
    dx = hprev - h
    mix = mix_ref[...]
    h_ref[...] = h.astype(bf16)
    xr_ref[...] = (h + dx * mix[0:1]).astype(bf16)
    xk_ref[...] = (h + dx * mix[2:3]).astype(bf16)
    xv_ref[...] = (h + dx * mix[3:4]).astype(bf16)
    xw = h + dx * mix[1:2]
    xa = h + dx * mix[4:5]
    xg = h + dx * mix[5:6]
    wl = w0_ref[...] + _dot(jnp.tanh(_dot(xw, w1_ref[...])), w2_ref[...])
    w_log = -_softplus(-wl) - 0.5
    lw_ref[...] = -jnp.exp(w_log)
    a_ref[...] = jax.nn.sigmoid(a0_ref[...] + _dot(_dot(xa, a1_ref[...]), a2_ref[...]))
    gate_ref[...] = _dot(jax.nn.sigmoid(_dot(xg, g1_ref[...])), g2_ref[...])


def _pad_to(a, axis, mult):
    n = a.shape[axis]
    npad = -(-n // mult) * mult - n
    if npad == 0:
        return a
    pads = [(0, 0)] * a.ndim
    pads[axis] = (0, npad)
    return jnp.pad(a, pads)


def _premix(x, sc, sh, nw, mix, w1, w2, w0, a1, a2, a0, g1, g2, tm=256):
    B, S, D = x.shape
    R = w2.shape[1]
    w1p = _pad_to(w1, 1, LANES).astype(bf16)
    w2p = _pad_to(w2, 0, LANES).astype(bf16)
    a1p = _pad_to(a1, 1, LANES).astype(bf16)
    a2p = _pad_to(a2, 0, LANES).astype(bf16)
    g1p = g1.astype(bf16)
    g2p = g2.astype(bf16)
    mixp = _pad_to(mix, 0, SUBLANES)
    nb = tm // SUBLANES

    def full(a):
        return pl.BlockSpec(a.shape, lambda b, i: (0,) * a.ndim)

    row = lambda w, dt: (pl.BlockSpec((None, tm, w), lambda b, i: (b, i, 0)), jax.ShapeDtypeStruct((B, S, w), dt))
    outs = [row(D, bf16), row(D, bf16), row(D, bf16), row(D, bf16), row(R, f32), row(R, f32), row(R, f32)]
    per_b = pl.BlockSpec((None, 1, D), lambda b, i: (b, 0, 0))
    args = (x, x, sc, sh, nw.reshape(1, D), mixp, w1p, w2p, w0.reshape(1, R), a1p, a2p, a0.reshape(1, R), g1p, g2p)
    in_specs = [pl.BlockSpec((None, tm, D), lambda b, i: (b, i, 0)),
                pl.BlockSpec((None, SUBLANES, D), lambda b, i: (b, jnp.maximum(i * nb - 1, 0), 0)),
                per_b, per_b] + [full(a) for a in args[4:]]
    return pl.pallas_call(
        _premix_kernel,
        grid=(B, S // tm),
        in_specs=in_specs,
        out_specs=[o[0] for o in outs],
        out_shape=[o[1] for o in outs],
        compiler_params=_cparams(("parallel", "arbitrary")),
        name="premix",
    )(*args)


def _mm_kernel(a_ref, b_ref, o_ref):
    o_ref[...] = jnp.dot(a_ref[...], b_ref[...], preferred_element_type=f32)


def _mm(a, b, tm=512, tn=512):
    M, K = a.shape
    N = b.shape[1]
    tm = min(tm, M)
    tn = min(tn, N)
    assert M % tm == 0 and N % tn == 0
    return pl.pallas_call(
        _mm_kernel,
        grid=(M // tm, N // tn),
        in_specs=[pl.BlockSpec((tm, K), lambda i, j: (i, 0)),
                  pl.BlockSpec((K, tn), lambda i, j: (0, j))],
        out_specs=pl.BlockSpec((tm, tn), lambda i, j: (i, j)),
        out_shape=jax.ShapeDtypeStruct((M, N), f32),
        compiler_params=_cparams(("parallel", "arbitrary")),
        name="mm",
    )(a, b)


def _gdn_prep_kernel(x_ref, xp_ref, ba_ref, cw_ref, alog_ref, dtb_ref, q_ref, k_ref, v_ref, bg_ref):
    i = pl.program_id(1)
    x = x_ref[...]
    tm = x.shape[0]
    xp = jnp.where(i == 0, 0.0, xp_ref[...])
    cw = cw_ref[...]
    rows8 = _iota(xp.shape, 0)
    acc = x * cw[GDN_CONV - 1:GDN_CONV]
    for j in range(1, GDN_CONV):
        rolled = pltpu.roll(x, j, 0)
        head = jnp.where(rows8 < j, pltpu.roll(xp, j, 0), rolled[0:SUBLANES])
        shifted = jnp.concatenate([head, rolled[SUBLANES:]], axis=0)
        acc = acc + shifted * cw[GDN_CONV - 1 - j:GDN_CONV - j]
    y = acc * jax.nn.sigmoid(acc)

    def l2n(t):
        return t * lax.rsqrt(jnp.sum(t * t, axis=-1, keepdims=True) + 1e-6)

    for hq in range(GDN_QK_HEADS):
        sl = slice(hq * GDN_HEAD, (hq + 1) * GDN_HEAD)
        q_ref[:, sl] = l2n(y[:, sl]) * (GDN_HEAD ** -0.5)
        ks = slice(GDN_KEY_DIM + hq * GDN_HEAD, GDN_KEY_DIM + (hq + 1) * GDN_HEAD)
        k_ref[:, sl] = l2n(y[:, ks])
    v_ref[...] = y[:, 2 * GDN_KEY_DIM:]

    ba = ba_ref[...]
    lane = _iota(ba.shape, 1)
    beta = jax.nn.sigmoid(ba)
    gstep = -jnp.exp(alog_ref[...]) * _softplus(ba + dtb_ref[...])
    gstep = jnp.where((lane >= GDN_V_HEADS) & (lane < 2 * GDN_V_HEADS), gstep, 0.0)
    r = _iota((tm, tm), 0)
    c = _iota((tm, tm), 1)
    tri = jnp.where((c <= r) & (r // CHUNK == c // CHUNK), 1.0, 0.0).astype(f32)
    gcum = _dot_f32(tri, gstep)
    bg_ref[...] = jnp.where(lane < GDN_V_HEADS, beta, gcum)


def _gdn_prep(qkv, ba, conv_w, a_log, dt_bias, tm=256):
    B, S, _ = qkv.shape
    C2 = 2 * GDN_KEY_DIM + GDN_VAL_DIM
    cwp = _pad_to(conv_w, 0, SUBLANES)
    alog = jnp.zeros((1, LANES), f32).at[0, GDN_V_HEADS:2 * GDN_V_HEADS].set(a_log)
    dtb = jnp.zeros((1, LANES), f32).at[0, GDN_V_HEADS:2 * GDN_V_HEADS].set(dt_bias)
    nb = tm // SUBLANES
    row = lambda w: (pl.BlockSpec((None, tm, w), lambda b, i: (b, i, 0)), jax.ShapeDtypeStruct((B, S, w), f32))
    outs = [row(GDN_KEY_DIM), row(GDN_KEY_DIM), row(GDN_VAL_DIM), row(LANES)]
    return pl.pallas_call(
        _gdn_prep_kernel,
        grid=(B, S // tm),
        in_specs=[pl.BlockSpec((None, tm, C2), lambda b, i: (b, i, 0)),
                  pl.BlockSpec((None, SUBLANES, C2), lambda b, i: (b, jnp.maximum(i * nb - 1, 0), 0)),
                  pl.BlockSpec((None, tm, LANES), lambda b, i: (b, i, 0)),
                  pl.BlockSpec(cwp.shape, lambda b, i: (0, 0)),
                  pl.BlockSpec((1, LANES), lambda b, i: (0, 0)),
                  pl.BlockSpec((1, LANES), lambda b, i: (0, 0))],
        out_specs=[o[0] for o in outs],
        out_shape=[o[1] for o in outs],
        compiler_params=_cparams(("parallel", "arbitrary")),
        name="gdn_prep",
    )(qkv, qkv, ba, cwp, alog, dtb)


def _block_diag(y, nblk):
    C, W = y.shape
    w = W // nblk
    t = jnp.concatenate([y] * nblk, axis=0)
    r = _iota(t.shape, 0) // C
    c = _iota(t.shape, 1) // w
    return jnp.where(r == c, t, jnp.zeros_like(t))


def _cat_matmul(xcat, ycat, nblk):
    return _dot(xcat, _block_diag(ycat.astype(bf16), nblk))


def _unit_lower_inverse(lcat, nblk):
    C = lcat.shape[0]
    assert C // INV_BLOCK == 4
    t = _iota(lcat.shape, 0)
    j = _iota(lcat.shape, 1) % C
    eye = jnp.where(t == j, 1.0, 0.0).astype(f32)
    in_diag = (t // INV_BLOCK) == (j // INV_BLOCK)
    dpart = jnp.where(in_diag, lcat, 0.0)
    opart = lcat - dpart
    m = -dpart
    x = eye + m
    p = m
    levels = INV_BLOCK.bit_length() - 2
    for _ in range(levels):
        p = _cat_matmul(p, p, nblk)
        x = x + _cat_matmul(x, p, nblk)
    n = _cat_matmul(x, opart, nblk)
    n2 = _cat_matmul(n, n, nblk)
    n3 = _cat_matmul(n, n2, nblk)
    return _cat_matmul(eye - n + n2 - n3, x, nblk)


def _gdn_chunk_kernel(q_ref, k_ref, v_ref, z_ref, bg_ref, grow_ref, nw_ref, o_ref, state_ref):
    c_id = pl.program_id(1)

    @pl.when(c_id == 0)
    def _():
        state_ref[...] = jnp.zeros_like(state_ref)

    C = CHUNK
    H = GDN_V_HEADS
    rep = GDN_V_HEADS // GDN_QK_HEADS
    bg = bg_ref[...]
    grow_all = grow_ref[...]
    nw = nw_ref[...]
    q = [q_ref[:, h * GDN_HEAD:(h + 1) * GDN_HEAD] for h in range(GDN_QK_HEADS)]
    k = [k_ref[:, h * GDN_HEAD:(h + 1) * GDN_HEAD] for h in range(GDN_QK_HEADS)]
    kk = [_dot_nt(k[h], k[h]) for h in range(GDN_QK_HEADS)]
    qk = [_dot_nt(q[h], k[h]) for h in range(GDN_QK_HEADS)]
    G = 4
    for g0 in range(0, H, G):
        heads = list(range(g0, g0 + G))
        gcol = jnp.concatenate([jnp.broadcast_to(bg[:, H + h:H + h + 1], (C, C)) for h in heads], axis=1)
        bcol = jnp.concatenate([jnp.broadcast_to(bg[:, h:h + 1], (C, C)) for h in heads], axis=1)
        grow = grow_all[:, g0 * C:(g0 + G) * C]
        t = _iota((C, G * C), 0)
        j = _iota((C, G * C), 1) % C
        incl = j <= t
        strict = j < t
        decay = jnp.where(incl, jnp.exp(jnp.where(incl, gcol - grow, 0.0)), 0.0)
        kkcat = jnp.concatenate([kk[h // rep] for h in heads], axis=1)
        qkcat = jnp.concatenate([qk[h // rep] for h in heads], axis=1)
        lower = jnp.where(strict, kkcat * bcol * decay, 0.0)
        tinv = _unit_lower_inverse(lower, G)
        intra = qkcat * decay
        for idx, h in enumerate(heads):
            kh = k[h // rep]
            qh = q[h // rep]
            vh = v_ref[:, h * GDN_HEAD:(h + 1) * GDN_HEAD]
            g_c = bg[:, H + h:H + h + 1]
            beta = bg[:, h:h + 1]
            g_last = bg[C - 1:C, H + h:H + h + 1]
            kb = kh * beta
            rhs = jnp.concatenate([vh * beta, kb * jnp.exp(g_c)], axis=1)
            sol = _dot(tinv[:, idx * C:(idx + 1) * C], rhs)
            u = sol[:, :GDN_HEAD]
            w = sol[:, GDN_HEAD:]
            s = state_ref[h]
            v_new = u - _dot(w, s)
            o = _dot(qh * jnp.exp(g_c), s) + _dot(intra[:, idx * C:(idx + 1) * C], v_new)
            k_dec = kh * jnp.exp(g_last - g_c)
            state_ref[h] = s * jnp.exp(g_last) + _dot_tn(k_dec, v_new)
            on = o * lax.rsqrt(jnp.mean(o * o, axis=-1, keepdims=True) + NORM_EPS) * nw
            zh = z_ref[:, h * GDN_HEAD:(h + 1) * GDN_HEAD]
            o_ref[:, h * GDN_HEAD:(h + 1) * GDN_HEAD] = (on * (zh * jax.nn.sigmoid(zh))).astype(o_ref.dtype)


def _gdn_chunk(q, k, v, qkvz, bg, grow, norm_w):
    B, S, _ = q.shape
    nc = S // CHUNK
    zblk = COL_Z // GDN_VAL_DIM
    return pl.pallas_call(
        _gdn_chunk_kernel,
        grid=(B, nc),
        in_specs=[pl.BlockSpec((None, CHUNK, GDN_KEY_DIM), lambda b, c: (b, c, 0)),
                  pl.BlockSpec((None, CHUNK, GDN_KEY_DIM), lambda b, c: (b, c, 0)),
                  pl.BlockSpec((None, CHUNK, GDN_VAL_DIM), lambda b, c: (b, c, 0)),
                  pl.BlockSpec((None, CHUNK, GDN_VAL_DIM), lambda b, c: (b, c, zblk)),
                  pl.BlockSpec((None, CHUNK, LANES), lambda b, c: (b, c, 0)),
                  pl.BlockSpec((None, None, 1, GDN_V_HEADS * CHUNK), lambda b, c: (b, c, 0, 0)),
                  pl.BlockSpec((1, GDN_HEAD), lambda b, c: (0, 0))],
        out_specs=pl.BlockSpec((None, CHUNK, GDN_VAL_DIM), lambda b, c: (b, c, 0)),
        out_shape=jax.ShapeDtypeStruct((B, S, GDN_VAL_DIM), bf16),
        scratch_shapes=[pltpu.VMEM((GDN_V_HEADS, GDN_HEAD, GDN_HEAD), f32)],
        compiler_params=_cparams(("parallel", "arbitrary")),
        name="gdn_chunk",
    )(q, k, v, qkvz, bg, grow, norm_w.reshape(1, GDN_HEAD))


def _head_sum(x):
    parts = []
    for s in range(x.shape[1] // LANES):
        xs = x[:, s * LANES:(s + 1) * LANES]
        lo = _iota(xs.shape, 1) < RWKV_HEAD
        s0 = jnp.sum(jnp.where(lo, xs, 0.0), axis=-1, keepdims=True)
        s1 = jnp.sum(jnp.where(lo, 0.0, xs), axis=-1, keepdims=True)
        parts.append(jnp.where(lo, s0, s1))
    return jnp.concatenate(parts, axis=1)


def _rwkv_chunk_kernel(r_ref, k_ref, v_ref, lw_ref, a_ref, gate_ref, kk_ref, ka_ref, rk_ref, lnw_ref, lnb_ref,
                       o_ref, state_ref):
    c_id = pl.program_id(1)

    @pl.when(c_id == 0)
    def _():
        state_ref[...] = jnp.zeros_like(state_ref)

    C = CHUNK
    G = RWKV_GROUP
    W = G * RWKV_HEAD
    n_groups = r_ref.shape[1] // W
    tr = _iota((C, C), 0)
    tc = _iota((C, C), 1)
    tri_incl = jnp.where(tc <= tr, 1.0, 0.0).astype(f32)
    t = _iota((C, W), 0)
    j = _iota((C, W), 1) % C
    incl = j <= t
    strict = j < t
    sr = _iota((W, W), 0) // RWKV_HEAD
    scol = _iota((W, W), 1) // RWKV_HEAD
    same_head = sr == scol
    for g in range(n_groups):
        sl = slice(g * W, (g + 1) * W)
        r = r_ref[:, sl]
        kr = k_ref[:, sl]
        v = v_ref[:, sl]
        lw = lw_ref[:, sl]
        a = a_ref[:, sl]
        kkv = kr * kk_ref[:, sl]
        kk = kkv * lax.rsqrt(_head_sum(kkv * kkv) + 1e-6)
        kmod = kr * (1.0 + (a - 1.0) * ka_ref[:, sl])
        avec = -kk
        bvec = kk * a
        cw = _dot_f32(tri_incl, lw)
        cl = cw[C - 1:C, :]
        e_in = jnp.exp(cw)
        e_out = jnp.exp(-cw)
        e_end = jnp.exp(cl - cw)
        at = avec * jnp.exp(cw - lw)
        rt = r * e_in
        bt = bvec * e_out
        kt = kmod * e_out
        ar = jnp.concatenate([at, rt], axis=0).astype(bf16)
        fb = _dot_nt(ar, _block_diag(bt.astype(bf16), G))
        fk = _dot_nt(ar, _block_diag(kt.astype(bf16), G))
        lab = jnp.where(strict, fb[:C], 0.0)
        mrb = jnp.where(incl, fb[C:], 0.0)
        lak = jnp.where(strict, fk[:C], 0.0)
        mrk = jnp.where(incl, fk[C:], 0.0)
        tinv = _unit_lower_inverse(-lab, G)
        s = state_ref[g]
        ars = _dot_nt(ar, s)
        vbd = _block_diag(v.astype(bf16), G)
        z = ars[:C] + _dot(lak, vbd)
        u = _cat_matmul(tinv, z, G)
        y = ars[C:] + _cat_matmul(mrb, u, G) + _dot(mrk, vbd)
        uv = jnp.concatenate([u, v], axis=0)
        bk = jnp.concatenate([bvec * e_end, kmod * e_end], axis=0)
        state_ref[g] = jnp.where(same_head, s * jnp.exp(cl) + _dot_tn(uv, bk), 0.0)
        mu = _head_sum(y) * (1.0 / RWKV_HEAD)
        d = y - mu
        var = _head_sum(d * d) * (1.0 / RWKV_HEAD)
        yn = d * lax.rsqrt(var + RWKV_LN_EPS) * lnw_ref[:, sl] + lnb_ref[:, sl]
        bonus = _head_sum(r * kmod * rk_ref[:, sl]) * v
        o_ref[:, sl] = ((yn + bonus) * gate_ref[:, sl]).astype(o_ref.dtype)


def _rwkv_chunk(r, k, v, lw, a, gate, k_k, k_a, r_k, ln_w, ln_b):
    B, S, R = r.shape
    nc = S // CHUNK
    W = RWKV_GROUP * RWKV_HEAD
    row = pl.BlockSpec((None, CHUNK, R), lambda b, c: (b, c, 0))
    par = pl.BlockSpec((1, R), lambda b, c: (0, 0))
    params = [p.reshape(1, R) for p in (k_k, k_a, r_k, ln_w, ln_b)]
    return pl.pallas_call(
        _rwkv_chunk_kernel,
        grid=(B, nc),
        in_specs=[row] * 6 + [par] * 5,
        out_specs=row,
        out_shape=jax.ShapeDtypeStruct((B, S, R), bf16),
        scratch_shapes=[pltpu.VMEM((R // W, W, W), f32)],
        compiler_params=_cparams(("parallel", "arbitrary")),
        name="rwkv_chunk",
    )(r, k, v, lw, a, gate, *params)


def _merge_kernel(yg_ref, yr_ref, gl_ref, x_ref, gt_ref, nw_ref, wug_ref, wur_ref, wo_ref, o_ref):
    D = x_ref.shape[1]
    sg = jax.nn.sigmoid(gl_ref[...])
    merged = (sg[:, :D] * jnp.dot(yg_ref[...], wug_ref[...], preferred_element_type=f32)
              + sg[:, D:] * jnp.dot(yr_ref[...], wur_ref[...], preferred_element_type=f32))
    out = _dot(merged, wo_ref[...])
    normed = out * lax.rsqrt(jnp.mean(out * out, axis=-1, keepdims=True) + NORM_EPS) * nw_ref[...]
    o_ref[...] = x_ref[...] + gt_ref[...] * normed


def _merge(yg, yr, gates, x, gt, nw, wug, wur, wo, tm=256):
    B, S, D = x.shape
    full = lambda a: pl.BlockSpec(a.shape, lambda b, i: (0,) * a.ndim)
    row = lambda w: pl.BlockSpec((None, tm, w), lambda b, i: (b, i, 0))
    wug, wur, wo = wug.astype(bf16), wur.astype(bf16), wo.astype(bf16)
    nw = nw.reshape(1, D)
    return pl.pallas_call(
        _merge_kernel,
        grid=(B, S // tm),
        in_specs=[row(yg.shape[2]), row(yr.shape[2]), row(2 * D), row(D),
                  pl.BlockSpec((None, 1, D), lambda b, i: (b, 0, 0)), full(nw), full(wug), full(wur), full(wo)],
        out_specs=row(D),
        out_shape=jax.ShapeDtypeStruct((B, S, D), f32),
        compiler_params=_cparams(("parallel", "arbitrary")),
        name="merge",
    )(yg, yr, gates, x, gt, nw, wug, wur, wo)


def _router_kernel(x_ref, sc_ref, sh_ref, nw_ref, whi_ref, wlo_ref, rb_ref, h_ref, sel_ref, prob_ref, cnt_ref,
                   carry_ref, *, n_experts):
    step = pl.program_id(0)

    @pl.when(step == 0)
    def _():
        carry_ref[...] = jnp.zeros_like(carry_ref)

    x = x_ref[...]
    tm = x.shape[0]
    ms = jnp.mean(x * x, axis=-1, keepdims=True)
    h = x * lax.rsqrt(ms + NORM_EPS) * (nw_ref[...] * (1.0 + sc_ref[...])) + sh_ref[...]
    h_ref[...] = h
    hi = h.astype(bf16)
    lo = (h - hi.astype(f32)).astype(bf16)
    whi = whi_ref[...]
    logits = (jnp.dot(hi, whi, preferred_element_type=f32) + jnp.dot(lo, whi, preferred_element_type=f32)
              + jnp.dot(hi, wlo_ref[...], preferred_element_type=f32)) + rb_ref[...]
    lane = _iota(logits.shape, 1)
    neg = jnp.float32(-jnp.inf)
    cur = jnp.where(lane < n_experts, logits, neg)
    vals, idxs = [], []
    for _ in range(TOP_K):
        m = jnp.max(cur, axis=-1, keepdims=True)
        ix = jnp.min(jnp.where(cur == m, lane, LANES), axis=-1, keepdims=True)
        vals.append(m)
        idxs.append(ix)
        cur = jnp.where(lane == ix, neg, cur)
    es = [jnp.exp(vv - vals[0]) for vv in vals]
    denom = es[0] + es[1] + es[2] + es[3]
    onehots = [lane == ix for ix in idxs]
    msel = jnp.zeros(logits.shape, f32)
    for oh in onehots:
        msel = msel + jnp.where(oh, 1.0, 0.0)
    r = _iota((tm, tm), 0)
    c = _iota((tm, tm), 1)
    tri = jnp.where(c < r, 1.0, 0.0).astype(bf16)
    carry = carry_ref[0:1, :]
    pref = jnp.dot(tri, msel.astype(bf16), preferred_element_type=f32) + carry
    sel = jnp.zeros(logits.shape, i32)
    prob = jnp.zeros(logits.shape, f32)
    for kk in range(TOP_K):
        rank = jnp.sum(jnp.where(onehots[kk], pref, 0.0), axis=-1, keepdims=True).astype(i32)
        sel = jnp.where(lane == kk, idxs[kk], sel)
        sel = jnp.where(lane == TOP_K + kk, rank, sel)
        prob = jnp.where(lane == kk, es[kk] / denom, prob)
    sel_ref[...] = sel
    prob_ref[...] = prob
    new_carry = carry + jnp.sum(msel, axis=0, keepdims=True)
    carry_ref[...] = jnp.broadcast_to(new_carry, carry_ref.shape)
    cnt_ref[...] = jnp.broadcast_to(new_carry, cnt_ref.shape)


def _router(x1, sc, sh, nw, rw, rb, tm=256):
    B, S, D = x1.shape
    E = rw.shape[1]
    rwp = _pad_to(rw, 1, LANES)
    whi = rwp.astype(bf16)
    wlo = (rwp - whi.astype(f32)).astype(bf16)
    rbp = _pad_to(rb.reshape(1, E), 1, LANES)
    spt = S // tm
    row = lambda w: pl.BlockSpec((None, tm, w), lambda i: (i // spt, i % spt, 0))
    per_b = pl.BlockSpec((None, 1, D), lambda i: (i // spt, 0, 0))
    full = lambda a: pl.BlockSpec(a.shape, lambda i: (0,) * a.ndim)
    nw = nw.reshape(1, D)
    return pl.pallas_call(
        functools.partial(_router_kernel, n_experts=E),
        grid=(B * spt,),
        in_specs=[row(D), per_b, per_b, full(nw), full(whi), full(wlo), full(rbp)],
        out_specs=[row(D), row(LANES), row(LANES), pl.BlockSpec((SUBLANES, LANES), lambda i: (0, 0))],
        out_shape=[jax.ShapeDtypeStruct((B, S, D), f32), jax.ShapeDtypeStruct((B, S, LANES), i32),
                   jax.ShapeDtypeStruct((B, S, LANES), f32), jax.ShapeDtypeStruct((SUBLANES, LANES), f32)],
        scratch_shapes=[pltpu.VMEM((SUBLANES, LANES), f32)],
        compiler_params=_cparams(("arbitrary",)),
        name="router",
    )(x1, sc, sh, nw, whi, wlo, rbp)


def _dispatch_kernel(pos_ref, h_ref, init_ref, xs_ref, sem):
    del init_ref
    tm = h_ref.shape[0]

    def row_copy(r, kk):
        p = pos_ref[0, r * TOP_K + kk]
        return pltpu.make_async_copy(h_ref.at[pl.ds(r, 1), :], xs_ref.at[pl.ds(p, 1), :], sem)

    def issue(r, carry):
        for kk in range(TOP_K):
            row_copy(r, kk).start()
        return carry

    def drain(r, carry):
        for kk in range(TOP_K):
            row_copy(r, kk).wait()
        return carry

    lax.fori_loop(0, tm, issue, 0)
    lax.fori_loop(0, tm, drain, 0)


def _dispatch(h2, pos, n_rows, tm=256):
    T, D = h2.shape
    pos3 = pos.reshape(T // tm, 1, tm * TOP_K)
    init = jnp.zeros((n_rows, D), h2.dtype)
    return pl.pallas_call(
        _dispatch_kernel,
        grid=(T // tm,),
        in_specs=[pl.BlockSpec((None, 1, tm * TOP_K), lambda i: (i, 0, 0), memory_space=pltpu.SMEM),
                  pl.BlockSpec((tm, D), lambda i: (i, 0)),
                  pl.BlockSpec(memory_space=pl.ANY)],
        out_specs=pl.BlockSpec(memory_space=pl.ANY),
        out_shape=jax.ShapeDtypeStruct((n_rows, D), h2.dtype),
        scratch_shapes=[pltpu.SemaphoreType.DMA],
        input_output_aliases={2: 0},
        compiler_params=_cparams(("arbitrary",)),
        name="dispatch",
    )(pos3, h2, init)


def _expert_kernel(be_ref, nu_ref, x_ref, wg_ref, wu_ref, bg_ref, bu_ref, wd_ref, bd_ref, o_ref, xb_ref):
    nb = pl.program_id(0)
    f = pl.program_id(1)

    @pl.when(nb < nu_ref[0])
    def _():
        @pl.when(f == 0)
        def _():
            xb_ref[...] = x_ref[...].astype(bf16)

        x = xb_ref[...]
        g = jnp.dot(x, wg_ref[...].astype(bf16), preferred_element_type=f32) + bg_ref[...]
        u = jnp.dot(x, wu_ref[...].astype(bf16), preferred_element_type=f32) + bu_ref[...]
        gl = jnp.minimum(g, SWIGLU_LIMIT)
        up = jnp.clip(u, -SWIGLU_LIMIT, SWIGLU_LIMIT)
        act = gl * jax.nn.sigmoid(SWIGLU_ALPHA * gl) * (up + 1.0)
        contrib = _dot(act, wd_ref[...])

        @pl.when(f == 0)
        def _():
            o_ref[...] = contrib + bd_ref[...]

        @pl.when(f > 0)
        def _():
            o_ref[...] += contrib

    @pl.when((nb >= nu_ref[0]) & (f == 0))
    def _():
        o_ref[...] = jnp.zeros_like(o_ref)


def _experts(xs, block_e, n_used, w_gu, b_gu, w_down, b_down):
    P, D = xs.shape
    E, _, F2 = w_gu.shape
    F = F2 // 2
    tf = EXPERT_FF_TILE
    nf = F // tf
    NB = P // EXPERT_ROWS
    b_gu3 = b_gu.reshape(E, 1, F2)
    b_dn3 = b_down.reshape(E, 1, D)

    def blk(nb, nu):
        return jnp.minimum(nb, nu[0] - 1)

    def fidx(nb, f, nu):
        return jnp.where(nb < nu[0], f, nf - 1)

    grid_spec = pltpu.PrefetchScalarGridSpec(
        num_scalar_prefetch=2,
        grid=(NB, nf),
        in_specs=[
            pl.BlockSpec((EXPERT_ROWS, D), lambda nb, f, be, nu: (blk(nb, nu), 0)),
            pl.BlockSpec((None, D, tf), lambda nb, f, be, nu: (be[blk(nb, nu)], 0, fidx(nb, f, nu))),
            pl.BlockSpec((None, D, tf), lambda nb, f, be, nu: (be[blk(nb, nu)], 0, nf + fidx(nb, f, nu))),
            pl.BlockSpec((None, 1, tf), lambda nb, f, be, nu: (be[blk(nb, nu)], 0, fidx(nb, f, nu))),
            pl.BlockSpec((None, 1, tf), lambda nb, f, be, nu: (be[blk(nb, nu)], 0, nf + fidx(nb, f, nu))),
            pl.BlockSpec((None, tf, D), lambda nb, f, be, nu: (be[blk(nb, nu)], fidx(nb, f, nu), 0)),
            pl.BlockSpec((None, 1, D), lambda nb, f, be, nu: (be[blk(nb, nu)], 0, 0)),
        ],
        out_specs=pl.BlockSpec((EXPERT_ROWS, D), lambda nb, f, be, nu: (nb, 0)),
        scratch_shapes=[pltpu.VMEM((EXPERT_ROWS, D), bf16)],
    )
    return pl.pallas_call(
        _expert_kernel,
        grid_spec=grid_spec,
        out_shape=jax.ShapeDtypeStruct((P, D), f32),
        compiler_params=_cparams(("arbitrary", "arbitrary")),
        name="experts",
    )(block_e, n_used, xs, w_gu, w_gu, b_gu3, b_gu3, w_down, b_dn3)


def _combine_kernel(pos_ref, yb_ref, prob_ref, x_ref, gt_ref, nw_ref, o_ref, buf_ref, sem):
    tm = x_ref.shape[0]

    def row_copy(r, kk):
        p = pos_ref[0, r * TOP_K + kk]
        return pltpu.make_async_copy(yb_ref.at[pl.ds(p, 1), :], buf_ref.at[kk, pl.ds(r, 1), :], sem)

    def issue(r, carry):
        for kk in range(TOP_K):
            row_copy(r, kk).start()
        return carry

    def drain(r, carry):
        for kk in range(TOP_K):
            row_copy(r, kk).wait()
        return carry

    lax.fori_loop(0, tm, issue, 0)
    lax.fori_loop(0, tm, drain, 0)
    prob = prob_ref[...]
    y = buf_ref[0] * prob[:, 0:1]
    for kk in range(1, TOP_K):
        y = y + buf_ref[kk] * prob[:, kk:kk + 1]
    normed = y * lax.rsqrt(jnp.mean(y * y, axis=-1, keepdims=True) + NORM_EPS) * nw_ref[...]
    o_ref[...] = x_ref[...] + gt_ref[...] * normed


def _combine(yb, pos, prob, x1, gt, nw, tm=256):
    B, S, D = x1.shape
    T = B * S
    spt = S // tm
    pos3 = pos.reshape(T // tm, 1, tm * TOP_K)
    row = lambda w: pl.BlockSpec((None, tm, w), lambda i: (i // spt, i % spt, 0))
    nw = nw.reshape(1, D)
    return pl.pallas_call(
        _combine_kernel,
        grid=(T // tm,),
        in_specs=[pl.BlockSpec((None, 1, tm * TOP_K), lambda i: (i, 0, 0), memory_space=pltpu.SMEM),
                  pl.BlockSpec(memory_space=pl.ANY),
                  row(LANES), row(D),
                  pl.BlockSpec((None, 1, D), lambda i: (i // spt, 0, 0)),
                  pl.BlockSpec(nw.shape, lambda i: (0, 0))],
        out_specs=row(D),
        out_shape=jax.ShapeDtypeStruct((B, S, D), f32),
        scratch_shapes=[pltpu.VMEM((TOP_K, tm, D), f32), pltpu.SemaphoreType.DMA],
        compiler_params=_cparams(("arbitrary",)),
        name="combine",
    )(pos3, yb, prob, x1, gt, nw)


def _mixer(x, sc1, sh1, gt1, norm_pre, norm_post, w_in, gdn_conv, gdn_A_log, gdn_dt_bias, gdn_norm, rwkv_mix,
           rwkv_w0, rwkv_w1, rwkv_w2, rwkv_a0, rwkv_a1, rwkv_a2, rwkv_g1, rwkv_g2, rwkv_k_k, rwkv_k_a, rwkv_r_k,
           rwkv_ln_w, rwkv_ln_b, w_up_gdn, w_up_rwkv, w_out):
    B, S, D = x.shape
    T = B * S
    R = rwkv_w2.shape[1]
    col_a = COL_B + GDN_V_HEADS
    col_gate = col_a + GDN_V_HEADS
    col_r = col_gate + 2 * D
    h, xr, xk, xv, lw, a_sig, gate = _premix(x, sc1, sh1, norm_pre, rwkv_mix, rwkv_w1, rwkv_w2, rwkv_w0, rwkv_a1,
                                             rwkv_a2, rwkv_a0, rwkv_g1, rwkv_g2)
    wb = w_in.astype(bf16)
    h2d = h.reshape(T, D)
    qkvz = _mm(h2d, wb[:, :COL_B]).reshape(B, S, COL_B)
    ba = _mm(h2d, _pad_to(wb[:, COL_B:col_gate], 1, LANES)).reshape(B, S, LANES)
    gates = _mm(h2d, wb[:, col_gate:col_r]).reshape(B, S, 2 * D)
    r = _mm(xr.reshape(T, D), wb[:, col_r:col_r + R]).reshape(B, S, R)
    k_r = _mm(xk.reshape(T, D), wb[:, col_r + R:col_r + 2 * R]).reshape(B, S, R)
    v_r = _mm(xv.reshape(T, D), wb[:, col_r + 2 * R:col_r + 3 * R]).reshape(B, S, R)

    q, k, v, bg = _gdn_prep(qkvz, ba, gdn_conv, gdn_A_log, gdn_dt_bias)
    nc = S // CHUNK
    grow = bg[:, :, GDN_V_HEADS:2 * GDN_V_HEADS].reshape(B, nc, CHUNK, GDN_V_HEADS)
    grow = grow.transpose(0, 1, 3, 2).reshape(B, nc, 1, GDN_V_HEADS * CHUNK)
    y_gdn = _gdn_chunk(q, k, v, qkvz, bg, grow, gdn_norm)
    y_rwkv = _rwkv_chunk(r, k_r, v_r, lw, a_sig, gate, rwkv_k_k, rwkv_k_a, rwkv_r_k.reshape(-1), rwkv_ln_w,
                         rwkv_ln_b)
    return _merge(y_gdn, y_rwkv, gates, x, gt1, norm_post, w_up_gdn, w_up_rwkv, w_out)


def _moe(x1, sc2, sh2, gt2, norm_pre, norm_post, router_w, router_b, w_gu, b_gu, w_down, b_down):
    B, S, D = x1.shape
    T = B * S
    E = router_w.shape[1]
    h2, sel, prob, cnt = _router(x1, sc2, sh2, norm_pre, router_w, router_b)
    sel = sel.reshape(T, LANES)
    e_idx = sel[:, :TOP_K]
    rank = sel[:, TOP_K:2 * TOP_K]
    counts = cnt[0, :E].astype(i32)
    padded = (counts + EXPERT_ROWS - 1) // EXPERT_ROWS * EXPERT_ROWS
    pad_end = jnp.cumsum(padded)
    pad_start = pad_end - padded
    pos = (pad_start[e_idx] + rank).astype(i32)
    P = T * TOP_K + E * EXPERT_ROWS
    NB = P // EXPERT_ROWS
    block_start = jnp.arange(NB, dtype=i32) * EXPERT_ROWS
    block_e = jnp.minimum(jnp.sum(pad_end[None, :] <= block_start[:, None], axis=1), E - 1).astype(i32)
    n_used = jnp.maximum(pad_end[-1:] // EXPERT_ROWS, 1).astype(i32)
    xs = _dispatch(h2.reshape(T, D), pos, P)
    yb = _experts(xs, block_e, n_used, w_gu, b_gu, w_down, b_down)
    return _combine(yb, pos, prob, x1, gt2, norm_post)


def kernel(x, c, ada_w, ada_b, norm_mix_pre, norm_mix_post, norm_ffn_pre, norm_ffn_post, w_in, gdn_conv, gdn_A_log,
           gdn_dt_bias, gdn_norm, rwkv_mix, rwkv_w0, rwkv_w1, rwkv_w2, rwkv_a0, rwkv_a1, rwkv_a2, rwkv_g1, rwkv_g2,
           rwkv_k_k, rwkv_k_a, rwkv_r_k, rwkv_ln_w, rwkv_ln_b, w_up_gdn, w_up_rwkv, w_out, router_w, router_b,
           exp_w_gu, exp_b_gu, exp_w_down, exp_b_down):
    B, S, D = x.shape
    for l in range(ada_w.shape[0]):
        mod = _ada(c, ada_w[l], ada_b[l])
        sh1, sc1, gt1, sh2, sc2, gt2 = [m.reshape(B, 1, D) for m in jnp.split(mod, 6, axis=-1)]
        x = _mixer(x, sc1, sh1, gt1, norm_mix_pre[l], norm_mix_post[l], w_in[l], gdn_conv[l], gdn_A_log[l],
                   gdn_dt_bias[l], gdn_norm[l], rwkv_mix[l], rwkv_w0[l], rwkv_w1[l], rwkv_w2[l], rwkv_a0[l],
                   rwkv_a1[l], rwkv_a2[l], rwkv_g1[l], rwkv_g2[l], rwkv_k_k[l], rwkv_k_a[l], rwkv_r_k[l],
                   rwkv_ln_w[l], rwkv_ln_b[l], w_up_gdn[l], w_up_rwkv[l], w_out[l])
        x = _moe(x, sc2, sh2, gt2, norm_ffn_pre[l], norm_ffn_post[l], router_w[l], router_b[l], exp_w_gu[l],
                 exp_b_gu[l], exp_w_down[l], exp_b_down[l])
    return x
```

```python
import functools

import jax
import jax.numpy as jnp
from jax import lax
from jax.experimental import pallas as pl
from jax.experimental.pallas import tpu as pltpu

f32 = jnp.float32
bf16 = jnp.bfloat16
i32 = jnp.int32

NORM_EPS = 1e-6
CHUNK = 64
GDN_QK_HEADS = 4
GDN_V_HEADS = 8
GDN_HEAD = 128
GDN_KEY_DIM = GDN_QK_HEADS * GDN_HEAD
GDN_VAL_DIM = GDN_V_HEADS * GDN_HEAD
GDN_CONV = 4
RWKV_HEAD = 64
RWKV_LN_EPS = 64e-5
RWKV_GROUP = 4
TOP_K = 4
SWIGLU_ALPHA = 1.702
SWIGLU_LIMIT = 7.0
COL_Z = 2 * GDN_KEY_DIM + GDN_VAL_DIM
COL_B = COL_Z + GDN_VAL_DIM

LANES = 128
SUBLANES = 8
VMEM_LIMIT_BYTES = 60 * 1024 * 1024

EXPERT_ROWS = 1024
EXPERT_SUB_ROWS = 256
EXPERT_FF_TILE = 256
INV_BLOCK = 16


def _cparams(sem):
    return pltpu.CompilerParams(dimension_semantics=sem, vmem_limit_bytes=VMEM_LIMIT_BYTES)


def _dot(a, b):
    return jnp.dot(a.astype(bf16), b.astype(bf16), preferred_element_type=f32)


def _dot_nt(a, b):
    return lax.dot_general(a.astype(bf16), b.astype(bf16), (((1,), (1,)), ((), ())), preferred_element_type=f32)


def _dot_tn(a, b):
    return lax.dot_general(a.astype(bf16), b.astype(bf16), (((0,), (0,)), ((), ())), preferred_element_type=f32)


def _dot_f32(a, b):
    return jnp.dot(a, b, preferred_element_type=f32, precision=lax.Precision.HIGHEST)


def _softplus(z):
    return jnp.maximum(z, 0.0) + jnp.log1p(jnp.exp(-jnp.abs(z)))


def _iota(shape, axis):
    return lax.broadcasted_iota(i32, shape, axis)


def _pack_bf16_pairs(h):
    half = h.shape[1] // 2
    lo = lax.bitcast_convert_type(h[:, :half].astype(bf16).astype(f32), jnp.uint32) >> 16
    hi = lax.bitcast_convert_type(h[:, half:].astype(bf16).astype(f32), jnp.uint32) & jnp.uint32(0xFFFF0000)
    return lo | hi


def _unpack_bf16_pairs(u):
    lo = lax.bitcast_convert_type(u << 16, f32).astype(bf16)
    hi = lax.bitcast_convert_type(u & jnp.uint32(0xFFFF0000), f32).astype(bf16)
    return jnp.concatenate([lo, hi], axis=1)


def _ada_kernel(c_ref, w_ref, b_ref, o_ref):
    c = c_ref[...]
    s = c * jax.nn.sigmoid(c)
    o_ref[...] = _dot(s, w_ref[...]) + b_ref[...]


def _ada(c, w, b):
    B, D = c.shape
    N = w.shape[1]
    Bp = -(-B // SUBLANES) * SUBLANES
    cp = jnp.pad(c, ((0, Bp - B), (0, 0)))
    tn = 1024
    out = pl.pallas_call(
        _ada_kernel,
        grid=(N // tn,),
        in_specs=[pl.BlockSpec((Bp, D), lambda j: (0, 0)),
                  pl.BlockSpec((D, tn), lambda j: (0, j)),
                  pl.BlockSpec((1, tn), lambda j: (0, j))],
        out_specs=pl.BlockSpec((Bp, tn), lambda j: (0, j)),
        out_shape=jax.ShapeDtypeStruct((Bp, N), f32),
        compiler_params=_cparams(("arbitrary",)),
        name="ada",
    )(cp, w, b.reshape(1, N))
    return out[:B]


def _premix_kernel(x_ref, xp_ref, sc_ref, sh_ref, nw_ref, mix_ref, w1_ref, w2_ref, w0_ref, a1_ref, a2_ref, a0_ref,
                   g1_ref, g2_ref, h_ref, xr_ref, xk_ref, xv_ref, lw_ref, a_ref, gate_ref):
    i = pl.program_id(1)
    scale = nw_ref[...] * (1.0 + sc_ref[...])
    shift = sh_ref[...]

    def modulated_norm(x):
        ms = jnp.mean(x * x, axis=-1, keepdims=True)
        return x * lax.rsqrt(ms + NORM_EPS) * scale + shift

    h = modulated_norm(x_ref[...])
    hp = modulated_norm(xp_ref[...])[SUBLANES - 1:SUBLANES, :]
    hp = jnp.where(i == 0, 0.0, hp)
    rows = _iota(h.shape, 0)
    hprev = jnp.where(rows == 0, hp, pltpu.roll(h, 1, 0))
    dx = hprev - h
    mix = mix_ref[...]
    h_ref[...] = h.astype(bf16)
    xr_ref[...] = (h + dx * mix[0:1]).astype(bf16)
    xk_ref[...] = (h + dx * mix[2:3]).astype(bf16)
    xv_ref[...] = (h + dx * mix[3:4]).astype(bf16)
    xw = h + dx * mix[1:2]
    xa = h + dx * mix[4:5]
    xg = h + dx * mix[5:6]
    wl = w0_ref[...] + _dot(jnp.tanh(_dot(xw, w1_ref[...])), w2_ref[...])
    w_log = -_softplus(-wl) - 0.5
    lw_ref[...] = -jnp.exp(w_log)
    a_ref[...] = jax.nn.sigmoid(a0_ref[...] + _dot(_dot(xa, a1_ref[...]), a2_ref[...]))
    gate_ref[...] = _dot(jax.nn.sigmoid(_dot(xg, g1_ref[...])), g2_ref[...])


def _pad_to(a, axis, mult):
    n = a.shape[axis]
    npad = -(-n // mult) * mult - n
    if npad == 0:
        return a
    pads = [(0, 0)] * a.ndim
    pads[axis] = (0, npad)
    return jnp.pad(a, pads)


def _premix(x, sc, sh, nw, mix, w1, w2, w0, a1, a2, a0, g1, g2, tm=256):
    B, S, D = x.shape
    R = w2.shape[1]
    w1p = _pad_to(w1, 1, LANES).astype(bf16)
    w2p = _pad_to(w2, 0, LANES).astype(bf16)
    a1p = _pad_to(a1, 1, LANES).astype(bf16)
    a2p = _pad_to(a2, 0, LANES).astype(bf16)
    g1p = g1.astype(bf16)
    g2p = g2.astype(bf16)
    mixp = _pad_to(mix, 0, SUBLANES)
    nb = tm // SUBLANES

    def full(a):
        return pl.BlockSpec(a.shape, lambda b, i: (0,) * a.ndim)

    row = lambda w, dt: (pl.BlockSpec((None, tm, w), lambda b, i: (b, i, 0)), jax.ShapeDtypeStruct((B, S, w), dt))
    outs = [row(D, bf16), row(D, bf16), row(D, bf16), row(D, bf16), row(R, f32), row(R, f32), row(R, f32)]
    per_b = pl.BlockSpec((None, 1, D), lambda b, i: (b, 0, 0))
    args = (x, x, sc, sh, nw.reshape(1, D), mixp, w1p, w2p, w0.reshape(1, R), a1p, a2p, a0.reshape(1, R), g1p, g2p)
    in_specs = [pl.BlockSpec((None, tm, D), lambda b, i: (b, i, 0)),
                pl.BlockSpec((None, SUBLANES, D), lambda b, i: (b, jnp.maximum(i * nb - 1, 0), 0)),
                per_b, per_b] + [full(a) for a in args[4:]]
    return pl.pallas_call(
        _premix_kernel,
        grid=(B, S // tm),
        in_specs=in_specs,
        out_specs=[o[0] for o in outs],
        out_shape=[o[1] for o in outs],
        compiler_params=_cparams(("parallel", "arbitrary")),
        name="premix",
    )(*args)


def _mm_kernel(a_ref, b_ref, o_ref):
    o_ref[...] = jnp.dot(a_ref[...], b_ref[...], preferred_element_type=f32)


def _mm(a, b, tm=1024, tn=1024):
    M, K = a.shape
    N = b.shape[1]
    tm = min(tm, M)
    tn = min(tn, N)
    assert M % tm == 0 and N % tn == 0
    return pl.pallas_call(
        _mm_kernel,
        grid=(M // tm, N // tn),
        in_specs=[pl.BlockSpec((tm, K), lambda i, j: (i, 0)),
                  pl.BlockSpec((K, tn), lambda i, j: (0, j))],
        out_specs=pl.BlockSpec((tm, tn), lambda i, j: (i, j)),
        out_shape=jax.ShapeDtypeStruct((M, N), f32),
        compiler_params=_cparams(("parallel", "arbitrary")),
        name="mm",
    )(a, b)


def _gdn_prep_kernel(x_ref, xp_ref, ba_ref, cw_ref, alog_ref, dtb_ref, q_ref, k_ref, v_ref, bg_ref):
    i = pl.program_id(1)
    x = x_ref[...]
    tm = x.shape[0]
    xp = jnp.where(i == 0, 0.0, xp_ref[...])
    cw = cw_ref[...]
    rows8 = _iota(xp.shape, 0)
    acc = x * cw[GDN_CONV - 1:GDN_CONV]
    for j in range(1, GDN_CONV):
        rolled = pltpu.roll(x, j, 0)
        head = jnp.where(rows8 < j, pltpu.roll(xp, j, 0), rolled[0:SUBLANES])
        shifted = jnp.concatenate([head, rolled[SUBLANES:]], axis=0)
        acc = acc + shifted * cw[GDN_CONV - 1 - j:GDN_CONV - j]
    y = acc * jax.nn.sigmoid(acc)

    def l2n(t):
        return t * lax.rsqrt(jnp.sum(t * t, axis=-1, keepdims=True) + 1e-6)

    for hq in range(GDN_QK_HEADS):
        sl = slice(hq * GDN_HEAD, (hq + 1) * GDN_HEAD)
        q_ref[:, sl] = l2n(y[:, sl]) * (GDN_HEAD ** -0.5)
        ks = slice(GDN_KEY_DIM + hq * GDN_HEAD, GDN_KEY_DIM + (hq + 1) * GDN_HEAD)
        k_ref[:, sl] = l2n(y[:, ks])
    v_ref[...] = y[:, 2 * GDN_KEY_DIM:]

    ba = ba_ref[...]
    lane = _iota(ba.shape, 1)
    beta = jax.nn.sigmoid(ba)
    gstep = -jnp.exp(alog_ref[...]) * _softplus(ba + dtb_ref[...])
    gstep = jnp.where((lane >= GDN_V_HEADS) & (lane < 2 * GDN_V_HEADS), gstep, 0.0)
    r = _iota((tm, tm), 0)
    c = _iota((tm, tm), 1)
    tri = jnp.where((c <= r) & (r // CHUNK == c // CHUNK), 1.0, 0.0).astype(f32)
    gcum = _dot_f32(tri, gstep)
    bg_ref[...] = jnp.where(lane < GDN_V_HEADS, beta, gcum)


def _gdn_prep(qkv, ba, conv_w, a_log, dt_bias, tm=256):
    B, S, _ = qkv.shape
    C2 = 2 * GDN_KEY_DIM + GDN_VAL_DIM
    cwp = _pad_to(conv_w, 0, SUBLANES)
    alog = jnp.zeros((1, LANES), f32).at[0, GDN_V_HEADS:2 * GDN_V_HEADS].set(a_log)
    dtb = jnp.zeros((1, LANES), f32).at[0, GDN_V_HEADS:2 * GDN_V_HEADS].set(dt_bias)
    nb = tm // SUBLANES
    row = lambda w: (pl.BlockSpec((None, tm, w), lambda b, i: (b, i, 0)), jax.ShapeDtypeStruct((B, S, w), f32))
    outs = [row(GDN_KEY_DIM), row(GDN_KEY_DIM), row(GDN_VAL_DIM), row(LANES)]
    return pl.pallas_call(
        _gdn_prep_kernel,
        grid=(B, S // tm),
        in_specs=[pl.BlockSpec((None, tm, C2), lambda b, i: (b, i, 0)),
                  pl.BlockSpec((None, SUBLANES, C2), lambda b, i: (b, jnp.maximum(i * nb - 1, 0), 0)),
                  pl.BlockSpec((None, tm, LANES), lambda b, i: (b, i, 0)),
                  pl.BlockSpec(cwp.shape, lambda b, i: (0, 0)),
                  pl.BlockSpec((1, LANES), lambda b, i: (0, 0)),
                  pl.BlockSpec((1, LANES), lambda b, i: (0, 0))],
        out_specs=[o[0] for o in outs],
        out_shape=[o[1] for o in outs],
        compiler_params=_cparams(("parallel", "arbitrary")),
        name="gdn_prep",
    )(qkv, qkv, ba, cwp, alog, dtb)


def _block_diag(y, nblk):
    C, W = y.shape
    w = W // nblk
    t = jnp.concatenate([y] * nblk, axis=0)
    r = _iota(t.shape, 0) // C
    c = _iota(t.shape, 1) // w
    return jnp.where(r == c, t, jnp.zeros_like(t))


def _cat_matmul(xcat, ycat, nblk):
    return _dot(xcat, _block_diag(ycat.astype(bf16), nblk))


def _unit_lower_inverse(lcats, nblk):
    C = lcats[0].shape[0]
    assert C // INV_BLOCK == 4
    t = _iota(lcats[0].shape, 0)
    j = _iota(lcats[0].shape, 1) % C
    eye = jnp.where(t == j, 1.0, 0.0).astype(f32)
    in_diag = (t // INV_BLOCK) == (j // INV_BLOCK)
    dparts = [jnp.where(in_diag, l, 0.0) for l in lcats]
    oparts = [l - d for l, d in zip(lcats, dparts)]
    ps = [-d for d in dparts]
    xs = [eye + p for p in ps]
    levels = INV_BLOCK.bit_length() - 2
    for _ in range(levels):
        ps = [_cat_matmul(p, p, nblk) for p in ps]
        xs = [x + _cat_matmul(x, p, nblk) for x, p in zip(xs, ps)]
    ns = [_cat_matmul(x, o, nblk) for x, o in zip(xs, oparts)]
    n2s = [_cat_matmul(n, n, nblk) for n in ns]
    n3s = [_cat_matmul(n, n2, nblk) for n, n2 in zip(ns, n2s)]
    return [_cat_matmul(eye - n + n2 - n3, x, nblk) for n, n2, n3, x in zip(ns, n2s, n3s, xs)]


def _gdn_chunk_kernel(q_ref, k_ref, v_ref, z_ref, bg_ref, grow_ref, nw_ref, o_ref, state_ref):
    c_id = pl.program_id(1)

    @pl.when(c_id == 0)
    def _():
        state_ref[...] = jnp.zeros_like(state_ref)

    C = CHUNK
    H = GDN_V_HEADS
    HD = GDN_HEAD
    rep = GDN_V_HEADS // GDN_QK_HEADS
    G = 4
    bg = bg_ref[...]
    grow_all = grow_ref[...]
    nw = nw_ref[...]
    q = [q_ref[:, h * HD:(h + 1) * HD] for h in range(GDN_QK_HEADS)]
    k = [k_ref[:, h * HD:(h + 1) * HD] for h in range(GDN_QK_HEADS)]
    kk = [_dot_nt(k[h], k[h]) for h in range(GDN_QK_HEADS)]
    qk = [_dot_nt(q[h], k[h]) for h in range(GDN_QK_HEADS)]
    t = _iota((C, G * C), 0)
    j = _iota((C, G * C), 1) % C
    incl = j <= t
    strict = j < t
    groups = [list(range(g0, g0 + G)) for g0 in range(0, H, G)]
    lowers, intras = [], []
    for heads in groups:
        gcol = jnp.concatenate([jnp.broadcast_to(bg[:, H + h:H + h + 1], (C, C)) for h in heads], axis=1)
        bcol = jnp.concatenate([jnp.broadcast_to(bg[:, h:h + 1], (C, C)) for h in heads], axis=1)
        grow = grow_all[:, heads[0] * C:(heads[-1] + 1) * C]
        decay = jnp.where(incl, jnp.exp(jnp.where(incl, gcol - grow, 0.0)), 0.0)
        kkcat = jnp.concatenate([kk[h // rep] for h in heads], axis=1)
        qkcat = jnp.concatenate([qk[h // rep] for h in heads], axis=1)
        lowers.append(jnp.where(strict, kkcat * bcol * decay, 0.0))
        intras.append(qkcat * decay)
    tinvs = _unit_lower_inverse(lowers, G)
    hs = range(H)
    g_c = [bg[:, H + h:H + h + 1] for h in hs]
    beta = [bg[:, h:h + 1] for h in hs]
    g_last = [bg[C - 1:C, H + h:H + h + 1] for h in hs]
    vh = [v_ref[:, h * HD:(h + 1) * HD] for h in hs]
    kb = [k[h // rep] * beta[h] for h in hs]
    rhs = [jnp.concatenate([vh[h] * beta[h], kb[h] * jnp.exp(g_c[h])], axis=1) for h in hs]
    sol = [_dot(tinvs[h // G][:, (h % G) * C:(h % G + 1) * C], rhs[h]) for h in hs]
    s = [state_ref[h] for h in hs]
    v_new = [sol[h][:, :HD] - _dot(sol[h][:, HD:], s[h]) for h in hs]
    o = [_dot(q[h // rep] * jnp.exp(g_c[h]), s[h]) + _dot(intras[h // G][:, (h % G) * C:(h % G + 1) * C], v_new[h])
         for h in hs]
    s_new = [s[h] * jnp.exp(g_last[h]) + _dot_tn(k[h // rep] * jnp.exp(g_last[h] - g_c[h]), v_new[h]) for h in hs]
    for h in hs:
        state_ref[h] = s_new[h]
        on = o[h] * lax.rsqrt(jnp.mean(o[h] * o[h], axis=-1, keepdims=True) + NORM_EPS) * nw
        zh = z_ref[:, h * HD:(h + 1) * HD]
        o_ref[:, h * HD:(h + 1) * HD] = (on * (zh * jax.nn.sigmoid(zh))).astype(o_ref.dtype)


def _gdn_chunk(q, k, v, qkvz, bg, grow, norm_w):
    B, S, _ = q.shape
    nc = S // CHUNK
    zblk = COL_Z // GDN_VAL_DIM
    return pl.pallas_call(
        _gdn_chunk_kernel,
        grid=(B, nc),
        in_specs=[pl.BlockSpec((None, CHUNK, GDN_KEY_DIM), lambda b, c: (b, c, 0)),
                  pl.BlockSpec((None, CHUNK, GDN_KEY_DIM), lambda b, c: (b, c, 0)),
                  pl.BlockSpec((None, CHUNK, GDN_VAL_DIM), lambda b, c: (b, c, 0)),
                  pl.BlockSpec((None, CHUNK, GDN_VAL_DIM), lambda b, c: (b, c, zblk)),
                  pl.BlockSpec((None, CHUNK, LANES), lambda b, c: (b, c, 0)),
                  pl.BlockSpec((None, None, 1, GDN_V_HEADS * CHUNK), lambda b, c: (b, c, 0, 0)),
                  pl.BlockSpec((1, GDN_HEAD), lambda b, c: (0, 0))],
        out_specs=pl.BlockSpec((None, CHUNK, GDN_VAL_DIM), lambda b, c: (b, c, 0)),
        out_shape=jax.ShapeDtypeStruct((B, S, GDN_VAL_DIM), bf16),
        scratch_shapes=[pltpu.VMEM((GDN_V_HEADS, GDN_HEAD, GDN_HEAD), f32)],
        compiler_params=_cparams(("parallel", "arbitrary")),
        name="gdn_chunk",
    )(q, k, v, qkvz, bg, grow, norm_w.reshape(1, GDN_HEAD))


def _head_sum(x):
    parts = []
    for s in range(x.shape[1] // LANES):
        xs = x[:, s * LANES:(s + 1) * LANES]
        lo = _iota(xs.shape, 1) < RWKV_HEAD
        s0 = jnp.sum(jnp.where(lo, xs, 0.0), axis=-1, keepdims=True)
        s1 = jnp.sum(jnp.where(lo, 0.0, xs), axis=-1, keepdims=True)
        parts.append(jnp.where(lo, s0, s1))
    return jnp.concatenate(parts, axis=1)


def _rwkv_chunk_kernel(r_ref, k_ref, v_ref, lw_ref, a_ref, gate_ref, kk_ref, ka_ref, rk_ref, lnw_ref, lnb_ref,
                       o_ref, state_ref):
    c_id = pl.program_id(1)

    @pl.when(c_id == 0)
    def _():
        state_ref[...] = jnp.zeros_like(state_ref)

    C = CHUNK
    G = RWKV_GROUP
    W = G * RWKV_HEAD
    gs = range(r_ref.shape[1] // W)
    sl = [slice(g * W, (g + 1) * W) for g in gs]
    tr = _iota((C, C), 0)
    tc = _iota((C, C), 1)
    tri_incl = jnp.where(tc <= tr, 1.0, 0.0).astype(f32)
    t = _iota((C, W), 0)
    j = _iota((C, W), 1) % C
    incl = j <= t
    strict = j < t
    same_head = (_iota((W, W), 0) // RWKV_HEAD) == (_iota((W, W), 1) // RWKV_HEAD)
    r = [r_ref[:, sl[g]] for g in gs]
    kr = [k_ref[:, sl[g]] for g in gs]
    v = [v_ref[:, sl[g]] for g in gs]
    lw = [lw_ref[:, sl[g]] for g in gs]
    a = [a_ref[:, sl[g]] for g in gs]
    cw = [_dot_f32(tri_incl, lw[g]) for g in gs]
    kkv = [kr[g] * kk_ref[:, sl[g]] for g in gs]
    kk = [kkv[g] * lax.rsqrt(_head_sum(kkv[g] * kkv[g]) + 1e-6) for g in gs]
    kmod = [kr[g] * (1.0 + (a[g] - 1.0) * ka_ref[:, sl[g]]) for g in gs]
    bvec = [kk[g] * a[g] for g in gs]
    cl = [cw[g][C - 1:C, :] for g in gs]
    e_out = [jnp.exp(-cw[g]) for g in gs]
    e_end = [jnp.exp(cl[g] - cw[g]) for g in gs]
    ar = [jnp.concatenate([-kk[g] * jnp.exp(cw[g] - lw[g]), r[g] * jnp.exp(cw[g])], axis=0).astype(bf16) for g in gs]
    fb = [_dot_nt(ar[g], _block_diag((bvec[g] * e_out[g]).astype(bf16), G)) for g in gs]
    fk = [_dot_nt(ar[g], _block_diag((kmod[g] * e_out[g]).astype(bf16), G)) for g in gs]
    s = [state_ref[g] for g in gs]
    ars = [_dot_nt(ar[g], s[g]) for g in gs]
    vbd = [_block_diag(v[g].astype(bf16), G) for g in gs]
    tinv = _unit_lower_inverse([jnp.where(strict, -fb[g][:C], 0.0) for g in gs], G)
    z = [ars[g][:C] + _dot(jnp.where(strict, fk[g][:C], 0.0), vbd[g]) for g in gs]
    u = [_cat_matmul(tinv[g], z[g], G) for g in gs]
    y = [ars[g][C:] + _cat_matmul(jnp.where(incl, fb[g][C:], 0.0), u[g], G)
         + _dot(jnp.where(incl, fk[g][C:], 0.0), vbd[g]) for g in gs]
    s_new = [jnp.where(same_head,
                       s[g] * jnp.exp(cl[g]) + _dot_tn(jnp.concatenate([u[g], v[g]], axis=0),
                                                       jnp.concatenate([bvec[g] * e_end[g], kmod[g] * e_end[g]],
                                                                       axis=0)), 0.0) for g in gs]
    for g in gs:
        state_ref[g] = s_new[g]
        mu = _head_sum(y[g]) * (1.0 / RWKV_HEAD)
        d = y[g] - mu
        var = _head_sum(d * d) * (1.0 / RWKV_HEAD)
        yn = d * lax.rsqrt(var + RWKV_LN_EPS) * lnw_ref[:, sl[g]] + lnb_ref[:, sl[g]]
        bonus = _head_sum(r[g] * kmod[g] * rk_ref[:, sl[g]]) * v[g]
        o_ref[:, sl[g]] = ((yn + bonus) * gate_ref[:, sl[g]]).astype(o_ref.dtype)


def _rwkv_chunk(r, k, v, lw, a, gate, k_k, k_a, r_k, ln_w, ln_b):
    B, S, R = r.shape
    nc = S // CHUNK
    W = RWKV_GROUP * RWKV_HEAD
    row = pl.BlockSpec((None, CHUNK, R), lambda b, c: (b, c, 0))
    par = pl.BlockSpec((1, R), lambda b, c: (0, 0))
    params = [p.reshape(1, R) for p in (k_k, k_a, r_k, ln_w, ln_b)]
    return pl.pallas_call(
        _rwkv_chunk_kernel,
        grid=(B, nc),
        in_specs=[row] * 6 + [par] * 5,
        out_specs=row,
        out_shape=jax.ShapeDtypeStruct((B, S, R), bf16),
        scratch_shapes=[pltpu.VMEM((R // W, W, W), f32)],
        compiler_params=_cparams(("parallel", "arbitrary")),
        name="rwkv_chunk",
    )(r, k, v, lw, a, gate, *params)


def _merge_kernel(yg_ref, yr_ref, gl_ref, x_ref, gt_ref, nw_ref, wug_ref, wur_ref, wo_ref, o_ref):
    D = x_ref.shape[1]
    sg = jax.nn.sigmoid(gl_ref[...])
    merged = (sg[:, :D] * jnp.dot(yg_ref[...], wug_ref[...], preferred_element_type=f32)
              + sg[:, D:] * jnp.dot(yr_ref[...], wur_ref[...], preferred_element_type=f32))
    out = _dot(merged, wo_ref[...])
    normed = out * lax.rsqrt(jnp.mean(out * out, axis=-1, keepdims=True) + NORM_EPS) * nw_ref[...]
    o_ref[...] = x_ref[...] + gt_ref[...] * normed


def _merge(yg, yr, gates, x, gt, nw, wug, wur, wo, tm=256):
    B, S, D = x.shape
    full = lambda a: pl.BlockSpec(a.shape, lambda b, i: (0,) * a.ndim)
    row = lambda w: pl.BlockSpec((None, tm, w), lambda b, i: (b, i, 0))
    wug, wur, wo = wug.astype(bf16), wur.astype(bf16), wo.astype(bf16)
    nw = nw.reshape(1, D)
    return pl.pallas_call(
        _merge_kernel,
        grid=(B, S // tm),
        in_specs=[row(yg.shape[2]), row(yr.shape[2]), row(2 * D), row(D),
                  pl.BlockSpec((None, 1, D), lambda b, i: (b, 0, 0)), full(nw), full(wug), full(wur), full(wo)],
        out_specs=row(D),
        out_shape=jax.ShapeDtypeStruct((B, S, D), f32),
        compiler_params=_cparams(("parallel", "arbitrary")),
        name="merge",
    )(yg, yr, gates, x, gt, nw, wug, wur, wo)


def _router_kernel(x_ref, sc_ref, sh_ref, nw_ref, whi_ref, wlo_ref, rb_ref, h_ref, sel_ref, prob_ref, cnt_ref,
                   carry_ref, *, n_experts):
    step = pl.program_id(0)

    @pl.when(step == 0)
    def _():
        carry_ref[...] = jnp.zeros_like(carry_ref)

    x = x_ref[...]
    tm = x.shape[0]
    ms = jnp.mean(x * x, axis=-1, keepdims=True)
    h = x * lax.rsqrt(ms + NORM_EPS) * (nw_ref[...] * (1.0 + sc_ref[...])) + sh_ref[...]
    h_ref[...] = _pack_bf16_pairs(h)
    hi = h.astype(bf16)
    lo = (h - hi.astype(f32)).astype(bf16)
    whi = whi_ref[...]
    logits = (jnp.dot(hi, whi, preferred_element_type=f32) + jnp.dot(lo, whi, preferred_element_type=f32)
              + jnp.dot(hi, wlo_ref[...], preferred_element_type=f32)) + rb_ref[...]
    lane = _iota(logits.shape, 1)
    neg = jnp.float32(-jnp.inf)
    cur = jnp.where(lane < n_experts, logits, neg)
    vals, idxs = [], []
    for _ in range(TOP_K):
        m = jnp.max(cur, axis=-1, keepdims=True)
        ix = jnp.min(jnp.where(cur == m, lane, LANES), axis=-1, keepdims=True)
        vals.append(m)
        idxs.append(ix)
        cur = jnp.where(lane == ix, neg, cur)
    es = [jnp.exp(vv - vals[0]) for vv in vals]
    denom = es[0] + es[1] + es[2] + es[3]
    onehots = [lane == ix for ix in idxs]
    msel = jnp.zeros(logits.shape, f32)
    for oh in onehots:
        msel = msel + jnp.where(oh, 1.0, 0.0)
    r = _iota((tm, tm), 0)
    c = _iota((tm, tm), 1)
    tri = jnp.where(c < r, 1.0, 0.0).astype(bf16)
    carry = carry_ref[0:1, :]
    pref = jnp.dot(tri, msel.astype(bf16), preferred_element_type=f32) + carry
    sel = jnp.zeros(logits.shape, i32)
    prob = jnp.zeros(logits.shape, f32)
    for kk in range(TOP_K):
        rank = jnp.sum(jnp.where(onehots[kk], pref, 0.0), axis=-1, keepdims=True).astype(i32)
        sel = jnp.where(lane == kk, idxs[kk], sel)
        sel = jnp.where(lane == TOP_K + kk, rank, sel)
        prob = jnp.where(lane == kk, es[kk] / denom, prob)
    sel_ref[...] = sel
    prob_ref[...] = prob
    new_carry = carry + jnp.sum(msel, axis=0, keepdims=True)
    carry_ref[...] = jnp.broadcast_to(new_carry, carry_ref.shape)
    cnt_ref[...] = jnp.broadcast_to(new_carry, cnt_ref.shape)


def _router(x1, sc, sh, nw, rw, rb, tm=256):
    B, S, D = x1.shape
    E = rw.shape[1]
    rwp = _pad_to(rw, 1, LANES)
    whi = rwp.astype(bf16)
    wlo = (rwp - whi.astype(f32)).astype(bf16)
    rbp = _pad_to(rb.reshape(1, E), 1, LANES)
    spt = S // tm
    row = lambda w: pl.BlockSpec((None, tm, w), lambda i: (i // spt, i % spt, 0))
    per_b = pl.BlockSpec((None, 1, D), lambda i: (i // spt, 0, 0))
    full = lambda a: pl.BlockSpec(a.shape, lambda i: (0,) * a.ndim)
    nw = nw.reshape(1, D)
    return pl.pallas_call(
        functools.partial(_router_kernel, n_experts=E),
        grid=(B * spt,),
        in_specs=[row(D), per_b, per_b, full(nw), full(whi), full(wlo), full(rbp)],
        out_specs=[row(D // 2), row(LANES), row(LANES), pl.BlockSpec((SUBLANES, LANES), lambda i: (0, 0))],
        out_shape=[jax.ShapeDtypeStruct((B, S, D // 2), jnp.uint32), jax.ShapeDtypeStruct((B, S, LANES), i32),
                   jax.ShapeDtypeStruct((B, S, LANES), f32), jax.ShapeDtypeStruct((SUBLANES, LANES), f32)],
        scratch_shapes=[pltpu.VMEM((SUBLANES, LANES), f32)],
        compiler_params=_cparams(("arbitrary",)),
        name="router",
    )(x1, sc, sh, nw, whi, wlo, rbp)


def _dispatch_kernel(pos_ref, h_ref, init_ref, xs_ref, sem):
    del init_ref
    tm = h_ref.shape[0]

    def row_copy(r, kk):
        p = pos_ref[0, r * TOP_K + kk]
        return pltpu.make_async_copy(h_ref.at[pl.ds(r, 1), :], xs_ref.at[pl.ds(p, 1), :], sem)

    def issue(r, carry):
        for kk in range(TOP_K):
            row_copy(r, kk).start()
        return carry

    def drain(r, carry):
        for kk in range(TOP_K):
            row_copy(r, kk).wait()
        return carry

    lax.fori_loop(0, tm, issue, 0)
    lax.fori_loop(0, tm, drain, 0)


def _dispatch(h2, pos, n_rows, tm=256):
    T, D = h2.shape
    pos3 = pos.reshape(T // tm, 1, tm * TOP_K)
    init = jnp.zeros((n_rows, D), h2.dtype)
    return pl.pallas_call(
        _dispatch_kernel,
        grid=(T // tm,),
        in_specs=[pl.BlockSpec((None, 1, tm * TOP_K), lambda i: (i, 0, 0), memory_space=pltpu.SMEM),
                  pl.BlockSpec((tm, D), lambda i: (i, 0)),
                  pl.BlockSpec(memory_space=pl.ANY)],
        out_specs=pl.BlockSpec(memory_space=pl.ANY),
        out_shape=jax.ShapeDtypeStruct((n_rows, D), h2.dtype),
        scratch_shapes=[pltpu.SemaphoreType.DMA],
        input_output_aliases={2: 0},
        compiler_params=_cparams(("arbitrary",)),
        name="dispatch",
    )(pos3, h2, init)


def _expert_kernel(be_ref, nv_ref, nu_ref, x_ref, wg_ref, wu_ref, bg_ref, bu_ref, wd_ref, bd_ref, o_ref, xb_ref):
    del be_ref, nu_ref
    nb = pl.program_id(0)
    f = pl.program_id(1)
    n_valid = nv_ref[nb]

    @pl.when((n_valid > 0) & (f == 0))
    def _():
        xb_ref[...] = _unpack_bf16_pairs(x_ref[...])
        o_ref[...] = jnp.broadcast_to(bd_ref[...], o_ref.shape)

    @pl.when((n_valid == 0) & (f == 0))
    def _():
        o_ref[...] = jnp.zeros_like(o_ref)

    n_sub = (n_valid + EXPERT_SUB_ROWS - 1) // EXPERT_SUB_ROWS
    for ns in range(1, EXPERT_ROWS // EXPERT_SUB_ROWS + 1):
        rows = slice(0, ns * EXPERT_SUB_ROWS)

        @pl.when(n_sub == ns)
        def _():
            x = xb_ref[rows, :]
            g = jnp.dot(x, wg_ref[...].astype(bf16), preferred_element_type=f32) + bg_ref[...]
            u = jnp.dot(x, wu_ref[...].astype(bf16), preferred_element_type=f32) + bu_ref[...]
            gl = jnp.minimum(g, SWIGLU_LIMIT)
            up = jnp.clip(u, -SWIGLU_LIMIT, SWIGLU_LIMIT)
            act = gl * jax.nn.sigmoid(SWIGLU_ALPHA * gl) * (up + 1.0)
            o_ref[rows, :] += jnp.dot(act.astype(bf16), wd_ref[...].astype(bf16), preferred_element_type=f32)


def _experts(xs, block_e, n_valid, n_used, w_gu, b_gu, w_down, b_down):
    P, Dh = xs.shape
    D = 2 * Dh
    E, _, F2 = w_gu.shape
    F = F2 // 2
    tf = EXPERT_FF_TILE
    nf = F // tf
    NB = P // EXPERT_ROWS
    b_gu3 = b_gu.reshape(E, 1, F2)
    b_dn3 = b_down.reshape(E, 1, D)

    def blk(nb, nu):
        return jnp.minimum(nb, nu[0] - 1)

    def fidx(nb, f, nu):
        return jnp.where(nb < nu[0], f, nf - 1)

    grid_spec = pltpu.PrefetchScalarGridSpec(
        num_scalar_prefetch=3,
        grid=(NB, nf),
        in_specs=[
            pl.BlockSpec((EXPERT_ROWS, Dh), lambda nb, f, be, nv, nu: (blk(nb, nu), 0)),
            pl.BlockSpec((None, D, tf), lambda nb, f, be, nv, nu: (be[blk(nb, nu)], 0, fidx(nb, f, nu))),
            pl.BlockSpec((None, D, tf), lambda nb, f, be, nv, nu: (be[blk(nb, nu)], 0, nf + fidx(nb, f, nu))),
            pl.BlockSpec((None, 1, tf), lambda nb, f, be, nv, nu: (be[blk(nb, nu)], 0, fidx(nb, f, nu))),
            pl.BlockSpec((None, 1, tf), lambda nb, f, be, nv, nu: (be[blk(nb, nu)], 0, nf + fidx(nb, f, nu))),
            pl.BlockSpec((None, tf, D), lambda nb, f, be, nv, nu: (be[blk(nb, nu)], fidx(nb, f, nu), 0)),
            pl.BlockSpec((None, 1, D), lambda nb, f, be, nv, nu: (be[blk(nb, nu)], 0, 0)),
        ],
        out_specs=pl.BlockSpec((EXPERT_ROWS, D), lambda nb, f, be, nv, nu: (nb, 0)),
        scratch_shapes=[pltpu.VMEM((EXPERT_ROWS, D), bf16)],
    )
    return pl.pallas_call(
        _expert_kernel,
        grid_spec=grid_spec,
        out_shape=jax.ShapeDtypeStruct((P, D), f32),
        compiler_params=_cparams(("arbitrary", "arbitrary")),
        name="experts",
    )(block_e, n_valid, n_used, xs, w_gu, w_gu, b_gu3, b_gu3, w_down, b_dn3)


def _combine_kernel(pos_ref, yb_ref, prob_ref, x_ref, gt_ref, nw_ref, o_ref, buf_ref, sem):
    tm = x_ref.shape[0]

    def row_copy(r, kk):
        p = pos_ref[0, r * TOP_K + kk]
        return pltpu.make_async_copy(yb_ref.at[pl.ds(p, 1), :], buf_ref.at[kk, pl.ds(r, 1), :], sem)

    def issue(r, carry):
        for kk in range(TOP_K):
            row_copy(r, kk).start()
        return carry

    def drain(r, carry):
        for kk in range(TOP_K):
            row_copy(r, kk).wait()
        return carry

    lax.fori_loop(0, tm, issue, 0)
    lax.fori_loop(0, tm, drain, 0)
    prob = prob_ref[...]
    y = buf_ref[0] * prob[:, 0:1]
    for kk in range(1, TOP_K):
        y = y + buf_ref[kk] * prob[:, kk:kk + 1]
    normed = y * lax.rsqrt(jnp.mean(y * y, axis=-1, keepdims=True) + NORM_EPS) * nw_ref[...]
    o_ref[...] = x_ref[...] + gt_ref[...] * normed


def _combine(yb, pos, prob, x1, gt, nw, tm=256):
    B, S, D = x1.shape
    T = B * S
    spt = S // tm
    pos3 = pos.reshape(T // tm, 1, tm * TOP_K)
    row = lambda w: pl.BlockSpec((None, tm, w), lambda i: (i // spt, i % spt, 0))
    nw = nw.reshape(1, D)
    return pl.pallas_call(
        _combine_kernel,
        grid=(T // tm,),
        in_specs=[pl.BlockSpec((None, 1, tm * TOP_K), lambda i: (i, 0, 0), memory_space=pltpu.SMEM),
                  pl.BlockSpec(memory_space=pl.ANY),
                  row(LANES), row(D),
                  pl.BlockSpec((None, 1, D), lambda i: (i // spt, 0, 0)),
                  pl.BlockSpec(nw.shape, lambda i: (0, 0))],
        out_specs=row(D),
        out_shape=jax.ShapeDtypeStruct((B, S, D), f32),
        scratch_shapes=[pltpu.VMEM((TOP_K, tm, D), f32), pltpu.SemaphoreType.DMA],
        compiler_params=_cparams(("arbitrary",)),
        name="combine",
    )(pos3, yb, prob, x1, gt, nw)


def _mixer(x, sc1, sh1, gt1, norm_pre, norm_post, w_in, gdn_conv, gdn_A_log, gdn_dt_bias, gdn_norm, rwkv_mix,
           rwkv_w0, rwkv_w1, rwkv_w2, rwkv_a0, rwkv_a1, rwkv_a2, rwkv_g1, rwkv_g2, rwkv_k_k, rwkv_k_a, rwkv_r_k,
           rwkv_ln_w, rwkv_ln_b, w_up_gdn, w_up_rwkv, w_out):
    B, S, D = x.shape
    T = B * S
    R = rwkv_w2.shape[1]
    col_a = COL_B + GDN_V_HEADS
    col_gate = col_a + GDN_V_HEADS
    col_r = col_gate + 2 * D
    h, xr, xk, xv, lw, a_sig, gate = _premix(x, sc1, sh1, norm_pre, rwkv_mix, rwkv_w1, rwkv_w2, rwkv_w0, rwkv_a1,
                                             rwkv_a2, rwkv_a0, rwkv_g1, rwkv_g2)
    wb = w_in.astype(bf16)
    h2d = h.reshape(T, D)
    qkvz = _mm(h2d, wb[:, :COL_B]).reshape(B, S, COL_B)
    ba = _mm(h2d, _pad_to(wb[:, COL_B:col_gate], 1, LANES)).reshape(B, S, LANES)
    gates = _mm(h2d, wb[:, col_gate:col_r]).reshape(B, S, 2 * D)
    r = _mm(xr.reshape(T, D), wb[:, col_r:col_r + R]).reshape(B, S, R)
    k_r = _mm(xk.reshape(T, D), wb[:, col_r + R:col_r + 2 * R]).reshape(B, S, R)
    v_r = _mm(xv.reshape(T, D), wb[:, col_r + 2 * R:col_r + 3 * R]).reshape(B, S, R)

    q, k, v, bg = _gdn_prep(qkvz, ba, gdn_conv, gdn_A_log, gdn_dt_bias)
    nc = S // CHUNK
    grow = bg[:, :, GDN_V_HEADS:2 * GDN_V_HEADS].reshape(B, nc, CHUNK, GDN_V_HEADS)
    grow = grow.transpose(0, 1, 3, 2).reshape(B, nc, 1, GDN_V_HEADS * CHUNK)
    y_gdn = _gdn_chunk(q, k, v, qkvz, bg, grow, gdn_norm)
    y_rwkv = _rwkv_chunk(r, k_r, v_r, lw, a_sig, gate, rwkv_k_k, rwkv_k_a, rwkv_r_k.reshape(-1), rwkv_ln_w,
                         rwkv_ln_b)
    return _merge(y_gdn, y_rwkv, gates, x, gt1, norm_post, w_up_gdn, w_up_rwkv, w_out)


def _moe(x1, sc2, sh2, gt2, norm_pre, norm_post, router_w, router_b, w_gu, b_gu, w_down, b_down):
    B, S, D = x1.shape
    T = B * S
    E = router_w.shape[1]
    h2, sel, prob, cnt = _router(x1, sc2, sh2, norm_pre, router_w, router_b)
    sel = sel.reshape(T, LANES)
    e_idx = sel[:, :TOP_K]
    rank = sel[:, TOP_K:2 * TOP_K]
    counts = cnt[0, :E].astype(i32)
    padded = (counts + EXPERT_ROWS - 1) // EXPERT_ROWS * EXPERT_ROWS
    pad_end = jnp.cumsum(padded)
    pad_start = pad_end - padded
    pos = (pad_start[e_idx] + rank).astype(i32)
    P = T * TOP_K + E * EXPERT_ROWS
    NB = P // EXPERT_ROWS
    block_start = jnp.arange(NB, dtype=i32) * EXPERT_ROWS
    block_e = jnp.minimum(jnp.sum(pad_end[None, :] <= block_start[:, None], axis=1), E - 1).astype(i32)
    n_valid = jnp.clip(pad_start[block_e] + counts[block_e] - block_start, 0, EXPERT_ROWS).astype(i32)
    n_used = jnp.maximum(pad_end[-1:] // EXPERT_ROWS, 1).astype(i32)
    xs = _dispatch(h2.reshape(T, D // 2), pos, P)
    yb = _experts(xs, block_e, n_valid, n_used, w_gu, b_gu, w_down, b_down)
    return _combine(yb, pos, prob, x1, gt2, norm_post)


def kernel(x, c, ada_w, ada_b, norm_mix_pre, norm_mix_post, norm_ffn_pre, norm_ffn_post, w_in, gdn_conv, gdn_A_log,
           gdn_dt_bias, gdn_norm, rwkv_mix, rwkv_w0, rwkv_w1, rwkv_w2, rwkv_a0, rwkv_a1, rwkv_a2, rwkv_g1, rwkv_g2,
           rwkv_k_k, rwkv_k_a, rwkv_r_k, rwkv_ln_w, rwkv_ln_b, w_up_gdn, w_up_rwkv, w_out, router_w, router_b,
           exp_w_gu, exp_b_gu, exp_w_down, exp_b_down):
    B, S, D = x.shape
    for l in range(ada_w.shape[0]):
        mod = _ada(c, ada_w[l], ada_b[l])
        sh1, sc1, gt1, sh2, sc2, gt2 = [m.reshape(B, 1, D) for m in jnp.split(mod, 6, axis=-1)]
        x = _mixer(x, sc1, sh1, gt1, norm_mix_pre[l], norm_mix_post[l], w_in[l], gdn_conv[l], gdn_A_log[l],
                   gdn_dt_bias[l], gdn_norm[l], rwkv_mix[l], rwkv_w0[l], rwkv_w1[l], rwkv_w2[l], rwkv_a0[l],
                   rwkv_a1[l], rwkv_a2[l], rwkv_g1[l], rwkv_g2[l], rwkv_k_k[l], rwkv_k_a[l], rwkv_r_k[l],
                   rwkv_ln_w[l], rwkv_ln_b[l], w_up_gdn[l], w_up_rwkv[l], w_out[l])
        x = _moe(x, sc2, sh2, gt2, norm_ffn_pre[l], norm_ffn_post[l], router_w[l], router_b[l], exp_w_gu[l],
                 exp_b_gu[l], exp_w_down[l], exp_b_down[l])
    return x
```

```python
import functools

import jax
import jax.numpy as jnp
from jax import lax
from jax.experimental import pallas as pl
from jax.experimental.pallas import tpu as pltpu

f32 = jnp.float32
bf16 = jnp.bfloat16
i32 = jnp.int32

NORM_EPS = 1e-6
CHUNK = 64
GDN_QK_HEADS = 4
GDN_V_HEADS = 8
GDN_HEAD = 128
GDN_KEY_DIM = GDN_QK_HEADS * GDN_HEAD
GDN_VAL_DIM = GDN_V_HEADS * GDN_HEAD
GDN_CONV = 4
RWKV_HEAD = 64
RWKV_LN_EPS = 64e-5
RWKV_GROUP = 4
TOP_K = 4
SWIGLU_ALPHA = 1.702
SWIGLU_LIMIT = 7.0
COL_Z = 2 * GDN_KEY_DIM + GDN_VAL_DIM
COL_B = COL_Z + GDN_VAL_DIM

LANES = 128
SUBLANES = 8
VMEM_LIMIT_BYTES = 60 * 1024 * 1024

EXPERT_ROWS = 1024
EXPERT_SUB_ROWS = 256
EXPERT_FF_TILE = 256
INV_BLOCK = 16


def _cparams(sem):
    return pltpu.CompilerParams(dimension_semantics=sem, vmem_limit_bytes=VMEM_LIMIT_BYTES)


def _dot(a, b):
    return jnp.dot(a.astype(bf16), b.astype(bf16), preferred_element_type=f32)


def _dot_nt(a, b):
    return lax.dot_general(a.astype(bf16), b.astype(bf16), (((1,), (1,)), ((), ())), preferred_element_type=f32)


def _dot_tn(a, b):
    return lax.dot_general(a.astype(bf16), b.astype(bf16), (((0,), (0,)), ((), ())), preferred_element_type=f32)


def _dot_f32(a, b):
    return jnp.dot(a, b, preferred_element_type=f32, precision=lax.Precision.HIGHEST)


def _softplus(z):
    return jnp.maximum(z, 0.0) + jnp.log1p(jnp.exp(-jnp.abs(z)))


def _iota(shape, axis):
    return lax.broadcasted_iota(i32, shape, axis)


def _pack_bf16_pairs(h):
    half = h.shape[1] // 2
    lo = lax.bitcast_convert_type(h[:, :half].astype(bf16).astype(f32), jnp.uint32) >> 16
    hi = lax.bitcast_convert_type(h[:, half:].astype(bf16).astype(f32), jnp.uint32) & jnp.uint32(0xFFFF0000)
    return lo | hi


def _unpack_bf16_pairs(u):
    lo = lax.bitcast_convert_type(u << 16, f32).astype(bf16)
    hi = lax.bitcast_convert_type(u & jnp.uint32(0xFFFF0000), f32).astype(bf16)
    return jnp.concatenate([lo, hi], axis=1)


def _ada_kernel(c_ref, w_ref, b_ref, o_ref):
    c = c_ref[...]
    s = c * jax.nn.sigmoid(c)
    o_ref[...] = _dot(s, w_ref[...]) + b_ref[...]


def _ada(c, w, b):
    B, D = c.shape
    N = w.shape[1]
    Bp = -(-B // SUBLANES) * SUBLANES
    cp = jnp.pad(c, ((0, Bp - B), (0, 0)))
    tn = 1024
    out = pl.pallas_call(
        _ada_kernel,
        grid=(N // tn,),
        in_specs=[pl.BlockSpec((Bp, D), lambda j: (0, 0)),
                  pl.BlockSpec((D, tn), lambda j: (0, j)),
                  pl.BlockSpec((1, tn), lambda j: (0, j))],
        out_specs=pl.BlockSpec((Bp, tn), lambda j: (0, j)),
        out_shape=jax.ShapeDtypeStruct((Bp, N), f32),
        compiler_params=_cparams(("arbitrary",)),
        name="ada",
    )(cp, w, b.reshape(1, N))
    return out[:B]


def _premix_kernel(x_ref, xp_ref, sc_ref, sh_ref, nw_ref, mix_ref, w1_ref, w2_ref, w0_ref, a1_ref, a2_ref, a0_ref,
                   g1_ref, g2_ref, h_ref, xr_ref, xk_ref, xv_ref, lw_ref, a_ref, gate_ref):
    i = pl.program_id(1)
    scale = nw_ref[...] * (1.0 + sc_ref[...])
    shift = sh_ref[...]

    def modulated_norm(x):
        ms = jnp.mean(x * x, axis=-1, keepdims=True)
        return x * lax.rsqrt(ms + NORM_EPS) * scale + shift

    h = modulated_norm(x_ref[...])
    hp = modulated_norm(xp_ref[...])[SUBLANES - 1:SUBLANES, :]
    hp = jnp.where(i == 0, 0.0, hp)
    rows = _iota(h.shape, 0)
    hprev = jnp.where(rows == 0, hp, pltpu.roll(h, 1, 0))
    dx = hprev - h
    mix = mix_ref[...]
    h_ref[...] = h.astype(bf16)
    xr_ref[...] = (h + dx * mix[0:1]).astype(bf16)
    xk_ref[...] = (h + dx * mix[2:3]).astype(bf16)
    xv_ref[...] = (h + dx * mix[3:4]).astype(bf16)
    xw = h + dx * mix[1:2]
    xa = h + dx * mix[4:5]
    xg = h + dx * mix[5:6]
    wl = w0_ref[...] + _dot(jnp.tanh(_dot(xw, w1_ref[...])), w2_ref[...])
    w_log = -_softplus(-wl) - 0.5
    lw_ref[...] = -jnp.exp(w_log)
    a_ref[...] = jax.nn.sigmoid(a0_ref[...] + _dot(_dot(xa, a1_ref[...]), a2_ref[...]))
    gate_ref[...] = _dot(jax.nn.sigmoid(_dot(xg, g1_ref[...])), g2_ref[...])


def _pad_to(a, axis, mult):
    n = a.shape[axis]
    npad = -(-n // mult) * mult - n
    if npad == 0:
        return a
    pads = [(0, 0)] * a.ndim
    pads[axis] = (0, npad)
    return jnp.pad(a, pads)


def _premix(x, sc, sh, nw, mix, w1, w2, w0, a1, a2, a0, g1, g2, tm=256):
    B, S, D = x.shape
    R = w2.shape[1]
    w1p = _pad_to(w1, 1, LANES).astype(bf16)
    w2p = _pad_to(w2, 0, LANES).astype(bf16)
    a1p = _pad_to(a1, 1, LANES).astype(bf16)
    a2p = _pad_to(a2, 0, LANES).astype(bf16)
    g1p = g1.astype(bf16)
    g2p = g2.astype(bf16)
    mixp = _pad_to(mix, 0, SUBLANES)
    nb = tm // SUBLANES

    def full(a):
        return pl.BlockSpec(a.shape, lambda b, i: (0,) * a.ndim)

    row = lambda w, dt: (pl.BlockSpec((None, tm, w), lambda b, i: (b, i, 0)), jax.ShapeDtypeStruct((B, S, w), dt))
    outs = [row(D, bf16), row(D, bf16), row(D, bf16), row(D, bf16), row(R, f32), row(R, f32), row(R, f32)]
    per_b = pl.BlockSpec((None, 1, D), lambda b, i: (b, 0, 0))
    args = (x, x, sc, sh, nw.reshape(1, D), mixp, w1p, w2p, w0.reshape(1, R), a1p, a2p, a0.reshape(1, R), g1p, g2p)
    in_specs = [pl.BlockSpec((None, tm, D), lambda b, i: (b, i, 0)),
                pl.BlockSpec((None, SUBLANES, D), lambda b, i: (b, jnp.maximum(i * nb - 1, 0), 0)),
                per_b, per_b] + [full(a) for a in args[4:]]
    return pl.pallas_call(
        _premix_kernel,
        grid=(B, S // tm),
        in_specs=in_specs,
        out_specs=[o[0] for o in outs],
        out_shape=[o[1] for o in outs],
        compiler_params=_cparams(("parallel", "arbitrary")),
        name="premix",
    )(*args)


def _mm_kernel(a_ref, b_ref, o_ref):
    o_ref[...] = jnp.dot(a_ref[...], b_ref[...], preferred_element_type=f32)


def _mm(a, b, tm=1024, tn=1024):
    M, K = a.shape
    N = b.shape[1]
    tm = min(tm, M)
    tn = min(tn, N)
    assert M % tm == 0 and N % tn == 0
    return pl.pallas_call(
        _mm_kernel,
        grid=(M // tm, N // tn),
        in_specs=[pl.BlockSpec((tm, K), lambda i, j: (i, 0)),
                  pl.BlockSpec((K, tn), lambda i, j: (0, j))],
        out_specs=pl.BlockSpec((tm, tn), lambda i, j: (i, j)),
        out_shape=jax.ShapeDtypeStruct((M, N), f32),
        compiler_params=_cparams(("parallel", "arbitrary")),
        name="mm",
    )(a, b)


def _gdn_prep_kernel(x_ref, xp_ref, ba_ref, cw_ref, alog_ref, dtb_ref, q_ref, k_ref, v_ref, bg_ref):
    i = pl.program_id(1)
    x = x_ref[...]
    tm = x.shape[0]
    xp = jnp.where(i == 0, 0.0, xp_ref[...])
    cw = cw_ref[...]
    rows8 = _iota(xp.shape, 0)
    acc = x * cw[GDN_CONV - 1:GDN_CONV]
    for j in range(1, GDN_CONV):
        rolled = pltpu.roll(x, j, 0)
        head = jnp.where(rows8 < j, pltpu.roll(xp, j, 0), rolled[0:SUBLANES])
        shifted = jnp.concatenate([head, rolled[SUBLANES:]], axis=0)
        acc = acc + shifted * cw[GDN_CONV - 1 - j:GDN_CONV - j]
    y = acc * jax.nn.sigmoid(acc)

    def l2n(t):
        return t * lax.rsqrt(jnp.sum(t * t, axis=-1, keepdims=True) + 1e-6)

    for hq in range(GDN_QK_HEADS):
        sl = slice(hq * GDN_HEAD, (hq + 1) * GDN_HEAD)
        q_ref[:, sl] = l2n(y[:, sl]) * (GDN_HEAD ** -0.5)
        ks = slice(GDN_KEY_DIM + hq * GDN_HEAD, GDN_KEY_DIM + (hq + 1) * GDN_HEAD)
        k_ref[:, sl] = l2n(y[:, ks])
    v_ref[...] = y[:, 2 * GDN_KEY_DIM:]

    ba = ba_ref[...]
    lane = _iota(ba.shape, 1)
    beta = jax.nn.sigmoid(ba)
    gstep = -jnp.exp(alog_ref[...]) * _softplus(ba + dtb_ref[...])
    gstep = jnp.where((lane >= GDN_V_HEADS) & (lane < 2 * GDN_V_HEADS), gstep, 0.0)
    r = _iota((tm, tm), 0)
    c = _iota((tm, tm), 1)
    tri = jnp.where((c <= r) & (r // CHUNK == c // CHUNK), 1.0, 0.0).astype(f32)
    gcum = _dot_f32(tri, gstep)
    bg_ref[...] = jnp.where(lane < GDN_V_HEADS, beta, gcum)


def _gdn_prep(qkv, ba, conv_w, a_log, dt_bias, tm=256):
    B, S, _ = qkv.shape
    C2 = 2 * GDN_KEY_DIM + GDN_VAL_DIM
    cwp = _pad_to(conv_w, 0, SUBLANES)
    alog = jnp.zeros((1, LANES), f32).at[0, GDN_V_HEADS:2 * GDN_V_HEADS].set(a_log)
    dtb = jnp.zeros((1, LANES), f32).at[0, GDN_V_HEADS:2 * GDN_V_HEADS].set(dt_bias)
    nb = tm // SUBLANES
    row = lambda w: (pl.BlockSpec((None, tm, w), lambda b, i: (b, i, 0)), jax.ShapeDtypeStruct((B, S, w), f32))
    outs = [row(GDN_KEY_DIM), row(GDN_KEY_DIM), row(GDN_VAL_DIM), row(LANES)]
    return pl.pallas_call(
        _gdn_prep_kernel,
        grid=(B, S // tm),
        in_specs=[pl.BlockSpec((None, tm, C2), lambda b, i: (b, i, 0)),
                  pl.BlockSpec((None, SUBLANES, C2), lambda b, i: (b, jnp.maximum(i * nb - 1, 0), 0)),
                  pl.BlockSpec((None, tm, LANES), lambda b, i: (b, i, 0)),
                  pl.BlockSpec(cwp.shape, lambda b, i: (0, 0)),
                  pl.BlockSpec((1, LANES), lambda b, i: (0, 0)),
                  pl.BlockSpec((1, LANES), lambda b, i: (0, 0))],
        out_specs=[o[0] for o in outs],
        out_shape=[o[1] for o in outs],
        compiler_params=_cparams(("parallel", "arbitrary")),
        name="gdn_prep",
    )(qkv, qkv, ba, cwp, alog, dtb)


def _seqs_per_step(batch):
    return next(n for n in (4, 2, 1) if batch % n == 0)


def _cumsum_rows(tri, x):
    hi = x.astype(bf16)
    r1 = x - hi.astype(f32)
    mid = r1.astype(bf16)
    lo = (r1 - mid.astype(f32)).astype(bf16)
    return (jnp.dot(tri, hi, preferred_element_type=f32) + jnp.dot(tri, mid, preferred_element_type=f32)
            + jnp.dot(tri, lo, preferred_element_type=f32))


def _block_diag(y, nblk):
    C, W = y.shape
    w = W // nblk
    t = jnp.concatenate([y] * nblk, axis=0)
    r = _iota(t.shape, 0) // C
    c = _iota(t.shape, 1) // w
    return jnp.where(r == c, t, jnp.zeros_like(t))


def _cat_matmul(xcat, ycat, nblk):
    return _dot(xcat, _block_diag(ycat.astype(bf16), nblk))


def _unit_lower_inverse(lcats, nblk):
    C = lcats[0].shape[0]
    assert C // INV_BLOCK == 4
    t = _iota(lcats[0].shape, 0)
    j = _iota(lcats[0].shape, 1) % C
    eye = jnp.where(t == j, 1.0, 0.0).astype(f32)
    in_diag = (t // INV_BLOCK) == (j // INV_BLOCK)
    dparts = [jnp.where(in_diag, l, 0.0) for l in lcats]
    oparts = [l - d for l, d in zip(lcats, dparts)]
    ps = [-d for d in dparts]
    xs = [eye + p for p in ps]
    levels = INV_BLOCK.bit_length() - 2
    for _ in range(levels):
        ps = [_cat_matmul(p, p, nblk) for p in ps]
        xs = [x + _cat_matmul(x, p, nblk) for x, p in zip(xs, ps)]
    ns = [_cat_matmul(x, o, nblk) for x, o in zip(xs, oparts)]
    n2s = [_cat_matmul(n, n, nblk) for n in ns]
    n3s = [_cat_matmul(n, n2, nblk) for n, n2 in zip(ns, n2s)]
    return [_cat_matmul(eye - n + n2 - n3, x, nblk) for n, n2, n3, x in zip(ns, n2s, n3s, xs)]


def _gdn_chunk_kernel(q_ref, k_ref, v_ref, z_ref, bg_ref, grow_ref, nw_ref, o_ref, state_ref):
    c_id = pl.program_id(1)

    @pl.when(c_id == 0)
    def _():
        state_ref[...] = jnp.zeros_like(state_ref)

    C = CHUNK
    H = GDN_V_HEADS
    HQ = GDN_QK_HEADS
    HD = GDN_HEAD
    rep = H // HQ
    G = 4
    NG = H // G
    seqs = range(q_ref.shape[0])
    nw = nw_ref[...]
    t = _iota((C, G * C), 0)
    j = _iota((C, G * C), 1) % C
    incl = j <= t
    strict = j < t
    bg = [bg_ref[bb] for bb in seqs]
    grow_all = [grow_ref[bb] for bb in seqs]
    qk_units = [(bb, h) for bb in seqs for h in range(HQ)]
    q = {(bb, h): q_ref[bb, :, h * HD:(h + 1) * HD] for bb, h in qk_units}
    k = {(bb, h): k_ref[bb, :, h * HD:(h + 1) * HD] for bb, h in qk_units}
    kk = {u: _dot_nt(k[u], k[u]) for u in qk_units}
    qk = {u: _dot_nt(q[u], k[u]) for u in qk_units}
    lowers, intras = [], []
    for bb in seqs:
        for g in range(NG):
            heads = range(g * G, (g + 1) * G)
            gcol = jnp.concatenate([jnp.broadcast_to(bg[bb][:, H + h:H + h + 1], (C, C)) for h in heads], axis=1)
            bcol = jnp.concatenate([jnp.broadcast_to(bg[bb][:, h:h + 1], (C, C)) for h in heads], axis=1)
            grow = grow_all[bb][:, g * G * C:(g + 1) * G * C]
            decay = jnp.where(incl, jnp.exp(jnp.where(incl, gcol - grow, 0.0)), 0.0)
            kkcat = jnp.concatenate([kk[(bb, h // rep)] for h in heads], axis=1)
            qkcat = jnp.concatenate([qk[(bb, h // rep)] for h in heads], axis=1)
            lowers.append(jnp.where(strict, kkcat * bcol * decay, 0.0))
            intras.append(qkcat * decay)
    tinvs = _unit_lower_inverse(lowers, G)

    def head_block(mats, bb, h):
        return mats[bb * NG + h // G][:, (h % G) * C:(h % G + 1) * C]

    units = [(bb, h) for bb in seqs for h in range(H)]
    g_c = {(bb, h): bg[bb][:, H + h:H + h + 1] for bb, h in units}
    beta = {(bb, h): bg[bb][:, h:h + 1] for bb, h in units}
    g_last = {(bb, h): bg[bb][C - 1:C, H + h:H + h + 1] for bb, h in units}
    kh = {(bb, h): k[(bb, h // rep)] for bb, h in units}
    qh = {(bb, h): q[(bb, h // rep)] for bb, h in units}
    rhs = {(bb, h): jnp.concatenate([v_ref[bb, :, h * HD:(h + 1) * HD] * beta[(bb, h)],
                                     kh[(bb, h)] * beta[(bb, h)] * jnp.exp(g_c[(bb, h)])], axis=1) for bb, h in units}
    sol = {(bb, h): _dot(head_block(tinvs, bb, h), rhs[(bb, h)]) for bb, h in units}
    s = {(bb, h): state_ref[bb * H + h] for bb, h in units}
    v_new = {u: sol[u][:, :HD] - _dot(sol[u][:, HD:], s[u]) for u in units}
    o = {(bb, h): _dot(qh[(bb, h)] * jnp.exp(g_c[(bb, h)]), s[(bb, h)])
         + _dot(head_block(intras, bb, h), v_new[(bb, h)]) for bb, h in units}
    s_new = {u: s[u] * jnp.exp(g_last[u]) + _dot_tn(kh[u] * jnp.exp(g_last[u] - g_c[u]), v_new[u]) for u in units}
    for bb, h in units:
        u = (bb, h)
        state_ref[bb * H + h] = s_new[u]
        on = o[u] * lax.rsqrt(jnp.mean(o[u] * o[u], axis=-1, keepdims=True) + NORM_EPS) * nw
        zh = z_ref[bb, :, h * HD:(h + 1) * HD]
        o_ref[bb, :, h * HD:(h + 1) * HD] = (on * (zh * jax.nn.sigmoid(zh))).astype(o_ref.dtype)


def _gdn_chunk(q, k, v, qkvz, bg, grow, norm_w):
    B, S, _ = q.shape
    nc = S // CHUNK
    zblk = COL_Z // GDN_VAL_DIM
    bb = _seqs_per_step(B)
    return pl.pallas_call(
        _gdn_chunk_kernel,
        grid=(B // bb, nc),
        in_specs=[pl.BlockSpec((bb, CHUNK, GDN_KEY_DIM), lambda b, c: (b, c, 0)),
                  pl.BlockSpec((bb, CHUNK, GDN_KEY_DIM), lambda b, c: (b, c, 0)),
                  pl.BlockSpec((bb, CHUNK, GDN_VAL_DIM), lambda b, c: (b, c, 0)),
                  pl.BlockSpec((bb, CHUNK, GDN_VAL_DIM), lambda b, c: (b, c, zblk)),
                  pl.BlockSpec((bb, CHUNK, LANES), lambda b, c: (b, c, 0)),
                  pl.BlockSpec((bb, None, 1, GDN_V_HEADS * CHUNK), lambda b, c: (b, c, 0, 0)),
                  pl.BlockSpec((1, GDN_HEAD), lambda b, c: (0, 0))],
        out_specs=pl.BlockSpec((bb, CHUNK, GDN_VAL_DIM), lambda b, c: (b, c, 0)),
        out_shape=jax.ShapeDtypeStruct((B, S, GDN_VAL_DIM), bf16),
        scratch_shapes=[pltpu.VMEM((bb * GDN_V_HEADS, GDN_HEAD, GDN_HEAD), f32)],
        compiler_params=_cparams(("parallel", "arbitrary")),
        name="gdn_chunk",
    )(q, k, v, qkvz, bg, grow, norm_w.reshape(1, GDN_HEAD))


def _head_sum(x):
    parts = []
    for s in range(x.shape[1] // LANES):
        xs = x[:, s * LANES:(s + 1) * LANES]
        lo = _iota(xs.shape, 1) < RWKV_HEAD
        s0 = jnp.sum(jnp.where(lo, xs, 0.0), axis=-1, keepdims=True)
        s1 = jnp.sum(jnp.where(lo, 0.0, xs), axis=-1, keepdims=True)
        parts.append(jnp.where(lo, s0, s1))
    return jnp.concatenate(parts, axis=1)


def _rwkv_chunk_kernel(r_ref, k_ref, v_ref, lw_ref, a_ref, gate_ref, kk_ref, ka_ref, rk_ref, lnw_ref, lnb_ref,
                       o_ref, state_ref):
    c_id = pl.program_id(1)

    @pl.when(c_id == 0)
    def _():
        state_ref[...] = jnp.zeros_like(state_ref)

    C = CHUNK
    G = RWKV_GROUP
    W = G * RWKV_HEAD
    n_groups = r_ref.shape[2] // W
    units = [(bb, g) for bb in range(r_ref.shape[0]) for g in range(n_groups)]
    sl = {(bb, g): slice(g * W, (g + 1) * W) for bb, g in units}
    tr = _iota((C, C), 0)
    tc = _iota((C, C), 1)
    tri_incl = jnp.where(tc <= tr, 1.0, 0.0).astype(bf16)
    t = _iota((C, W), 0)
    j = _iota((C, W), 1) % C
    incl = j <= t
    strict = j < t
    same_head = (_iota((W, W), 0) // RWKV_HEAD) == (_iota((W, W), 1) // RWKV_HEAD)
    r = {u: r_ref[u[0], :, sl[u]] for u in units}
    kr = {u: k_ref[u[0], :, sl[u]] for u in units}
    v = {u: v_ref[u[0], :, sl[u]] for u in units}
    lw = {u: lw_ref[u[0], :, sl[u]] for u in units}
    a = {u: a_ref[u[0], :, sl[u]] for u in units}
    cw = {u: _cumsum_rows(tri_incl, lw[u]) for u in units}
    kkv = {u: kr[u] * kk_ref[:, sl[u]] for u in units}
    kk = {u: kkv[u] * lax.rsqrt(_head_sum(kkv[u] * kkv[u]) + 1e-6) for u in units}
    kmod = {u: kr[u] * (1.0 + (a[u] - 1.0) * ka_ref[:, sl[u]]) for u in units}
    bvec = {u: kk[u] * a[u] for u in units}
    cl = {u: cw[u][C - 1:C, :] for u in units}
    e_out = {u: jnp.exp(-cw[u]) for u in units}
    e_end = {u: jnp.exp(cl[u] - cw[u]) for u in units}
    ar = {u: jnp.concatenate([-kk[u] * jnp.exp(cw[u] - lw[u]), r[u] * jnp.exp(cw[u])], axis=0).astype(bf16)
          for u in units}
    fb = {u: _dot_nt(ar[u], _block_diag((bvec[u] * e_out[u]).astype(bf16), G)) for u in units}
    fk = {u: _dot_nt(ar[u], _block_diag((kmod[u] * e_out[u]).astype(bf16), G)) for u in units}
    s = {(bb, g): state_ref[bb * n_groups + g] for bb, g in units}
    ars = {u: _dot_nt(ar[u], s[u]) for u in units}
    vbd = {u: _block_diag(v[u].astype(bf16), G) for u in units}
    tinv = dict(zip(units, _unit_lower_inverse([jnp.where(strict, -fb[u][:C], 0.0) for u in units], G)))
    z = {u: ars[u][:C] + _dot(jnp.where(strict, fk[u][:C], 0.0), vbd[u]) for u in units}
    uu = {u: _cat_matmul(tinv[u], z[u], G) for u in units}
    y = {u: ars[u][C:] + _cat_matmul(jnp.where(incl, fb[u][C:], 0.0), uu[u], G)
         + _dot(jnp.where(incl, fk[u][C:], 0.0), vbd[u]) for u in units}
    s_new = {u: jnp.where(same_head,
                          s[u] * jnp.exp(cl[u]) + _dot_tn(jnp.concatenate([uu[u], v[u]], axis=0),
                                                          jnp.concatenate([bvec[u] * e_end[u], kmod[u] * e_end[u]],
                                                                          axis=0)), 0.0) for u in units}
    for bb, g in units:
        u = (bb, g)
        state_ref[bb * n_groups + g] = s_new[u]
        mu = _head_sum(y[u]) * (1.0 / RWKV_HEAD)
        d = y[u] - mu
        var = _head_sum(d * d) * (1.0 / RWKV_HEAD)
        yn = d * lax.rsqrt(var + RWKV_LN_EPS) * lnw_ref[:, sl[u]] + lnb_ref[:, sl[u]]
        bonus = _head_sum(r[u] * kmod[u] * rk_ref[:, sl[u]]) * v[u]
        o_ref[bb, :, sl[u]] = ((yn + bonus) * gate_ref[bb, :, sl[u]]).astype(o_ref.dtype)


def _rwkv_chunk(r, k, v, lw, a, gate, k_k, k_a, r_k, ln_w, ln_b):
    B, S, R = r.shape
    nc = S // CHUNK
    W = RWKV_GROUP * RWKV_HEAD
    bb = _seqs_per_step(B)
    row = pl.BlockSpec((bb, CHUNK, R), lambda b, c: (b, c, 0))
    par = pl.BlockSpec((1, R), lambda b, c: (0, 0))
    params = [p.reshape(1, R) for p in (k_k, k_a, r_k, ln_w, ln_b)]
    return pl.pallas_call(
        _rwkv_chunk_kernel,
        grid=(B // bb, nc),
        in_specs=[row] * 6 + [par] * 5,
        out_specs=row,
        out_shape=jax.ShapeDtypeStruct((B, S, R), bf16),
        scratch_shapes=[pltpu.VMEM((bb * (R // W), W, W), f32)],
        compiler_params=_cparams(("parallel", "arbitrary")),
        name="rwkv_chunk",
    )(r, k, v, lw, a, gate, *params)


def _merge_kernel(yg_ref, yr_ref, gl_ref, x_ref, gt_ref, nw_ref, wug_ref, wur_ref, wo_ref, o_ref):
    D = x_ref.shape[1]
    sg = jax.nn.sigmoid(gl_ref[...])
    merged = (sg[:, :D] * jnp.dot(yg_ref[...], wug_ref[...], preferred_element_type=f32)
              + sg[:, D:] * jnp.dot(yr_ref[...], wur_ref[...], preferred_element_type=f32))
    out = _dot(merged, wo_ref[...])
    normed = out * lax.rsqrt(jnp.mean(out * out, axis=-1, keepdims=True) + NORM_EPS) * nw_ref[...]
    o_ref[...] = x_ref[...] + gt_ref[...] * normed


def _merge(yg, yr, gates, x, gt, nw, wug, wur, wo, tm=256):
    B, S, D = x.shape
    full = lambda a: pl.BlockSpec(a.shape, lambda b, i: (0,) * a.ndim)
    row = lambda w: pl.BlockSpec((None, tm, w), lambda b, i: (b, i, 0))
    wug, wur, wo = wug.astype(bf16), wur.astype(bf16), wo.astype(bf16)
    nw = nw.reshape(1, D)
    return pl.pallas_call(
        _merge_kernel,
        grid=(B, S // tm),
        in_specs=[row(yg.shape[2]), row(yr.shape[2]), row(2 * D), row(D),
                  pl.BlockSpec((None, 1, D), lambda b, i: (b, 0, 0)), full(nw), full(wug), full(wur), full(wo)],
        out_specs=row(D),
        out_shape=jax.ShapeDtypeStruct((B, S, D), f32),
        compiler_params=_cparams(("parallel", "arbitrary")),
        name="merge",
    )(yg, yr, gates, x, gt, nw, wug, wur, wo)


def _router_kernel(x_ref, sc_ref, sh_ref, nw_ref, whi_ref, wlo_ref, rb_ref, h_ref, sel_ref, prob_ref, cnt_ref,
                   carry_ref, *, n_experts):
    step = pl.program_id(0)

    @pl.when(step == 0)
    def _():
        carry_ref[...] = jnp.zeros_like(carry_ref)

    x = x_ref[...]
    tm = x.shape[0]
    ms = jnp.mean(x * x, axis=-1, keepdims=True)
    h = x * lax.rsqrt(ms + NORM_EPS) * (nw_ref[...] * (1.0 + sc_ref[...])) + sh_ref[...]
    h_ref[...] = _pack_bf16_pairs(h)
    hi = h.astype(bf16)
    lo = (h - hi.astype(f32)).astype(bf16)
    whi = whi_ref[...]
    logits = (jnp.dot(hi, whi, preferred_element_type=f32) + jnp.dot(lo, whi, preferred_element_type=f32)
              + jnp.dot(hi, wlo_ref[...], preferred_element_type=f32)) + rb_ref[...]
    lane = _iota(logits.shape, 1)
    neg = jnp.float32(-jnp.inf)
    cur = jnp.where(lane < n_experts, logits, neg)
    vals, idxs = [], []
    for _ in range(TOP_K):
        m = jnp.max(cur, axis=-1, keepdims=True)
        ix = jnp.min(jnp.where(cur == m, lane, LANES), axis=-1, keepdims=True)
        vals.append(m)
        idxs.append(ix)
        cur = jnp.where(lane == ix, neg, cur)
    es = [jnp.exp(vv - vals[0]) for vv in vals]
    denom = es[0] + es[1] + es[2] + es[3]
    onehots = [lane == ix for ix in idxs]
    msel = jnp.zeros(logits.shape, f32)
    for oh in onehots:
        msel = msel + jnp.where(oh, 1.0, 0.0)
    r = _iota((tm, tm), 0)
    c = _iota((tm, tm), 1)
    tri = jnp.where(c < r, 1.0, 0.0).astype(bf16)
    carry = carry_ref[0:1, :]
    pref = jnp.dot(tri, msel.astype(bf16), preferred_element_type=f32) + carry
    sel = jnp.zeros(logits.shape, i32)
    prob = jnp.zeros(logits.shape, f32)
    for kk in range(TOP_K):
        rank = jnp.sum(jnp.where(onehots[kk], pref, 0.0), axis=-1, keepdims=True).astype(i32)
        sel = jnp.where(lane == kk, idxs[kk], sel)
        sel = jnp.where(lane == TOP_K + kk, rank, sel)
        prob = jnp.where(lane == kk, es[kk] / denom, prob)
    sel_ref[...] = sel
    prob_ref[...] = prob
    new_carry = carry + jnp.sum(msel, axis=0, keepdims=True)
    carry_ref[...] = jnp.broadcast_to(new_carry, carry_ref.shape)
    cnt_ref[...] = jnp.broadcast_to(new_carry, cnt_ref.shape)


def _router(x1, sc, sh, nw, rw, rb, tm=256):
    B, S, D = x1.shape
    E = rw.shape[1]
    rwp = _pad_to(rw, 1, LANES)
    whi = rwp.astype(bf16)
    wlo = (rwp - whi.astype(f32)).astype(bf16)
    rbp = _pad_to(rb.reshape(1, E), 1, LANES)
    spt = S // tm
    row = lambda w: pl.BlockSpec((None, tm, w), lambda i: (i // spt, i % spt, 0))
    per_b = pl.BlockSpec((None, 1, D), lambda i: (i // spt, 0, 0))
    full = lambda a: pl.BlockSpec(a.shape, lambda i: (0,) * a.ndim)
    nw = nw.reshape(1, D)
    return pl.pallas_call(
        functools.partial(_router_kernel, n_experts=E),
        grid=(B * spt,),
        in_specs=[row(D), per_b, per_b, full(nw), full(whi), full(wlo), full(rbp)],
        out_specs=[row(D // 2), row(LANES), row(LANES), pl.BlockSpec((SUBLANES, LANES), lambda i: (0, 0))],
        out_shape=[jax.ShapeDtypeStruct((B, S, D // 2), jnp.uint32), jax.ShapeDtypeStruct((B, S, LANES), i32),
                   jax.ShapeDtypeStruct((B, S, LANES), f32), jax.ShapeDtypeStruct((SUBLANES, LANES), f32)],
        scratch_shapes=[pltpu.VMEM((SUBLANES, LANES), f32)],
        compiler_params=_cparams(("arbitrary",)),
        name="router",
    )(x1, sc, sh, nw, whi, wlo, rbp)


def _dispatch_kernel(pos_ref, h_ref, init_ref, xs_ref, sem):
    del init_ref
    tm = h_ref.shape[0]

    def row_copy(r, kk):
        p = pos_ref[0, r * TOP_K + kk]
        return pltpu.make_async_copy(h_ref.at[pl.ds(r, 1), :], xs_ref.at[pl.ds(p, 1), :], sem)

    def issue(r, carry):
        for kk in range(TOP_K):
            row_copy(r, kk).start(priority=kk % 2)
        return carry

    def drain(r, carry):
        for kk in range(TOP_K):
            row_copy(r, kk).wait()
        return carry

    lax.fori_loop(0, tm, issue, 0)
    lax.fori_loop(0, tm, drain, 0)


def _dispatch(h2, pos, n_rows, tm=256):
    T, D = h2.shape
    pos3 = pos.reshape(T // tm, 1, tm * TOP_K)
    init = jnp.zeros((n_rows, D), h2.dtype)
    return pl.pallas_call(
        _dispatch_kernel,
        grid=(T // tm,),
        in_specs=[pl.BlockSpec((None, 1, tm * TOP_K), lambda i: (i, 0, 0), memory_space=pltpu.SMEM),
                  pl.BlockSpec((tm, D), lambda i: (i, 0)),
                  pl.BlockSpec(memory_space=pl.ANY)],
        out_specs=pl.BlockSpec(memory_space=pl.ANY),
        out_shape=jax.ShapeDtypeStruct((n_rows, D), h2.dtype),
        scratch_shapes=[pltpu.SemaphoreType.DMA],
        input_output_aliases={2: 0},
        compiler_params=_cparams(("arbitrary",)),
        name="dispatch",
    )(pos3, h2, init)


def _expert_kernel(be_ref, nv_ref, nu_ref, x_ref, wg_ref, wu_ref, bg_ref, bu_ref, wd_ref, bd_ref, o_ref, xb_ref):
    del be_ref, nu_ref
    nb = pl.program_id(0)
    f = pl.program_id(1)
    n_valid = nv_ref[nb]

    @pl.when((n_valid > 0) & (f == 0))
    def _():
        xb_ref[...] = _unpack_bf16_pairs(x_ref[...])
        o_ref[...] = jnp.broadcast_to(bd_ref[...], o_ref.shape)

    @pl.when((n_valid == 0) & (f == 0))
    def _():
        o_ref[...] = jnp.zeros_like(o_ref)

    n_sub = (n_valid + EXPERT_SUB_ROWS - 1) // EXPERT_SUB_ROWS
    for ns in range(1, EXPERT_ROWS // EXPERT_SUB_ROWS + 1):
        rows = slice(0, ns * EXPERT_SUB_ROWS)

        @pl.when(n_sub == ns)
        def _():
            x = xb_ref[rows, :]
            g = jnp.dot(x, wg_ref[...].astype(bf16), preferred_element_type=f32) + bg_ref[...]
            u = jnp.dot(x, wu_ref[...].astype(bf16), preferred_element_type=f32) + bu_ref[...]
            gl = jnp.minimum(g, SWIGLU_LIMIT)
            up = jnp.clip(u, -SWIGLU_LIMIT, SWIGLU_LIMIT)
            act = gl * jax.nn.sigmoid(SWIGLU_ALPHA * gl) * (up + 1.0)
            o_ref[rows, :] += jnp.dot(act.astype(bf16), wd_ref[...].astype(bf16), preferred_element_type=f32)


def _experts(xs, block_e, n_valid, n_used, w_gu, b_gu, w_down, b_down):
    P, Dh = xs.shape
    D = 2 * Dh
    E, _, F2 = w_gu.shape
    F = F2 // 2
    tf = EXPERT_FF_TILE
    nf = F // tf
    NB = P // EXPERT_ROWS
    b_gu3 = b_gu.reshape(E, 1, F2)
    b_dn3 = b_down.reshape(E, 1, D)

    def blk(nb, nu):
        return jnp.minimum(nb, nu[0] - 1)

    def fidx(nb, f, nu):
        return jnp.where(nb < nu[0], f, nf - 1)

    grid_spec = pltpu.PrefetchScalarGridSpec(
        num_scalar_prefetch=3,
        grid=(NB, nf),
        in_specs=[
            pl.BlockSpec((EXPERT_ROWS, Dh), lambda nb, f, be, nv, nu: (blk(nb, nu), 0)),
            pl.BlockSpec((None, D, tf), lambda nb, f, be, nv, nu: (be[blk(nb, nu)], 0, fidx(nb, f, nu))),
            pl.BlockSpec((None, D, tf), lambda nb, f, be, nv, nu: (be[blk(nb, nu)], 0, nf + fidx(nb, f, nu))),
            pl.BlockSpec((None, 1, tf), lambda nb, f, be, nv, nu: (be[blk(nb, nu)], 0, fidx(nb, f, nu))),
            pl.BlockSpec((None, 1, tf), lambda nb, f, be, nv, nu: (be[blk(nb, nu)], 0, nf + fidx(nb, f, nu))),
            pl.BlockSpec((None, tf, D), lambda nb, f, be, nv, nu: (be[blk(nb, nu)], fidx(nb, f, nu), 0)),
            pl.BlockSpec((None, 1, D), lambda nb, f, be, nv, nu: (be[blk(nb, nu)], 0, 0)),
        ],
        out_specs=pl.BlockSpec((EXPERT_ROWS, D), lambda nb, f, be, nv, nu: (nb, 0)),
        scratch_shapes=[pltpu.VMEM((EXPERT_ROWS, D), bf16)],
    )
    return pl.pallas_call(
        _expert_kernel,
        grid_spec=grid_spec,
        out_shape=jax.ShapeDtypeStruct((P, D), f32),
        compiler_params=_cparams(("arbitrary", "arbitrary")),
        name="experts",
    )(block_e, n_valid, n_used, xs, w_gu, w_gu, b_gu3, b_gu3, w_down, b_dn3)


def _combine_kernel(pos_ref, yb_ref, prob_ref, x_ref, gt_ref, nw_ref, o_ref, buf_ref, sem):
    tm = x_ref.shape[0]

    def row_copy(r, kk):
        p = pos_ref[0, r * TOP_K + kk]
        return pltpu.make_async_copy(yb_ref.at[pl.ds(p, 1), :], buf_ref.at[kk, pl.ds(r, 1), :], sem)

    def issue(r, carry):
        for kk in range(TOP_K):
            row_copy(r, kk).start(priority=kk % 2)
        return carry

    def drain(r, carry):
        for kk in range(TOP_K):
            row_copy(r, kk).wait()
        return carry

    lax.fori_loop(0, tm, issue, 0)
    lax.fori_loop(0, tm, drain, 0)
    prob = prob_ref[...]
    y = buf_ref[0] * prob[:, 0:1]
    for kk in range(1, TOP_K):
        y = y + buf_ref[kk] * prob[:, kk:kk + 1]
    normed = y * lax.rsqrt(jnp.mean(y * y, axis=-1, keepdims=True) + NORM_EPS) * nw_ref[...]
    o_ref[...] = x_ref[...] + gt_ref[...] * normed


def _combine(yb, pos, prob, x1, gt, nw, tm=256):
    B, S, D = x1.shape
    T = B * S
    spt = S // tm
    pos3 = pos.reshape(T // tm, 1, tm * TOP_K)
    row = lambda w: pl.BlockSpec((None, tm, w), lambda i: (i // spt, i % spt, 0))
    nw = nw.reshape(1, D)
    return pl.pallas_call(
        _combine_kernel,
        grid=(T // tm,),
        in_specs=[pl.BlockSpec((None, 1, tm * TOP_K), lambda i: (i, 0, 0), memory_space=pltpu.SMEM),
                  pl.BlockSpec(memory_space=pl.ANY),
                  row(LANES), row(D),
                  pl.BlockSpec((None, 1, D), lambda i: (i // spt, 0, 0)),
                  pl.BlockSpec(nw.shape, lambda i: (0, 0))],
        out_specs=row(D),
        out_shape=jax.ShapeDtypeStruct((B, S, D), f32),
        scratch_shapes=[pltpu.VMEM((TOP_K, tm, D), f32), pltpu.SemaphoreType.DMA],
        compiler_params=_cparams(("arbitrary",)),
        name="combine",
    )(pos3, yb, prob, x1, gt, nw)


def _mixer(x, sc1, sh1, gt1, norm_pre, norm_post, w_in, gdn_conv, gdn_A_log, gdn_dt_bias, gdn_norm, rwkv_mix,
           rwkv_w0, rwkv_w1, rwkv_w2, rwkv_a0, rwkv_a1, rwkv_a2, rwkv_g1, rwkv_g2, rwkv_k_k, rwkv_k_a, rwkv_r_k,
           rwkv_ln_w, rwkv_ln_b, w_up_gdn, w_up_rwkv, w_out):
    B, S, D = x.shape
    T = B * S
    R = rwkv_w2.shape[1]
    col_a = COL_B + GDN_V_HEADS
    col_gate = col_a + GDN_V_HEADS
    col_r = col_gate + 2 * D
    h, xr, xk, xv, lw, a_sig, gate = _premix(x, sc1, sh1, norm_pre, rwkv_mix, rwkv_w1, rwkv_w2, rwkv_w0, rwkv_a1,
                                             rwkv_a2, rwkv_a0, rwkv_g1, rwkv_g2)
    wb = w_in.astype(bf16)
    h2d = h.reshape(T, D)
    qkvz = _mm(h2d, wb[:, :COL_B]).reshape(B, S, COL_B)
    ba = _mm(h2d, _pad_to(wb[:, COL_B:col_gate], 1, LANES)).reshape(B, S, LANES)
    gates = _mm(h2d, wb[:, col_gate:col_r]).reshape(B, S, 2 * D)
    r = _mm(xr.reshape(T, D), wb[:, col_r:col_r + R]).reshape(B, S, R)
    k_r = _mm(xk.reshape(T, D), wb[:, col_r + R:col_r + 2 * R]).reshape(B, S, R)
    v_r = _mm(xv.reshape(T, D), wb[:, col_r + 2 * R:col_r + 3 * R]).reshape(B, S, R)

    q, k, v, bg = _gdn_prep(qkvz, ba, gdn_conv, gdn_A_log, gdn_dt_bias)
    nc = S // CHUNK
    grow = bg[:, :, GDN_V_HEADS:2 * GDN_V_HEADS].reshape(B, nc, CHUNK, GDN_V_HEADS)
    grow = grow.transpose(0, 1, 3, 2).reshape(B, nc, 1, GDN_V_HEADS * CHUNK)
    y_gdn = _gdn_chunk(q, k, v, qkvz, bg, grow, gdn_norm)
    y_rwkv = _rwkv_chunk(r, k_r, v_r, lw, a_sig, gate, rwkv_k_k, rwkv_k_a, rwkv_r_k.reshape(-1), rwkv_ln_w,
                         rwkv_ln_b)
    return _merge(y_gdn, y_rwkv, gates, x, gt1, norm_post, w_up_gdn, w_up_rwkv, w_out)


def _moe(x1, sc2, sh2, gt2, norm_pre, norm_post, router_w, router_b, w_gu, b_gu, w_down, b_down):
    B, S, D = x1.shape
    T = B * S
    E = router_w.shape[1]
    h2, sel, prob, cnt = _router(x1, sc2, sh2, norm_pre, router_w, router_b)
    sel = sel.reshape(T, LANES)
    e_idx = sel[:, :TOP_K]
    rank = sel[:, TOP_K:2 * TOP_K]
    counts = cnt[0, :E].astype(i32)
    padded = (counts + EXPERT_ROWS - 1) // EXPERT_ROWS * EXPERT_ROWS
    pad_end = jnp.cumsum(padded)
    pad_start = pad_end - padded
    pos = (pad_start[e_idx] + rank).astype(i32)
    P = T * TOP_K + E * EXPERT_ROWS
    NB = P // EXPERT_ROWS
    block_start = jnp.arange(NB, dtype=i32) * EXPERT_ROWS
    block_e = jnp.minimum(jnp.sum(pad_end[None, :] <= block_start[:, None], axis=1), E - 1).astype(i32)
    n_valid = jnp.clip(pad_start[block_e] + counts[block_e] - block_start, 0, EXPERT_ROWS).astype(i32)
    n_used = jnp.maximum(pad_end[-1:] // EXPERT_ROWS, 1).astype(i32)
    xs = _dispatch(h2.reshape(T, D // 2), pos, P)
    yb = _experts(xs, block_e, n_valid, n_used, w_gu, b_gu, w_down, b_down)
    return _combine(yb, pos, prob, x1, gt2, norm_post)


def kernel(x, c, ada_w, ada_b, norm_mix_pre, norm_mix_post, norm_ffn_pre, norm_ffn_post, w_in, gdn_conv, gdn_A_log,
           gdn_dt_bias, gdn_norm, rwkv_mix, rwkv_w0, rwkv_w1, rwkv_w2, rwkv_a0, rwkv_a1, rwkv_a2, rwkv_g1, rwkv_g2,
           rwkv_k_k, rwkv_k_a, rwkv_r_k, rwkv_ln_w, rwkv_ln_b, w_up_gdn, w_up_rwkv, w_out, router_w, router_b,
           exp_w_gu, exp_b_gu, exp_w_down, exp_b_down):
    B, S, D = x.shape
    for l in range(ada_w.shape[0]):
        mod = _ada(c, ada_w[l], ada_b[l])
        sh1, sc1, gt1, sh2, sc2, gt2 = [m.reshape(B, 1, D) for m in jnp.split(mod, 6, axis=-1)]
        x = _mixer(x, sc1, sh1, gt1, norm_mix_pre[l], norm_mix_post[l], w_in[l], gdn_conv[l], gdn_A_log[l],
                   gdn_dt_bias[l], gdn_norm[l], rwkv_mix[l], rwkv_w0[l], rwkv_w1[l], rwkv_w2[l], rwkv_a0[l],
                   rwkv_a1[l], rwkv_a2[l], rwkv_g1[l], rwkv_g2[l], rwkv_k_k[l], rwkv_k_a[l], rwkv_r_k[l],
                   rwkv_ln_w[l], rwkv_ln_b[l], w_up_gdn[l], w_up_rwkv[l], w_out[l])
        x = _moe(x, sc2, sh2, gt2, norm_ffn_pre[l], norm_ffn_post[l], router_w[l], router_b[l], exp_w_gu[l],
                 exp_b_gu[l], exp_w_down[l], exp_b_down[l])
    return x
```

```python
import functools

import jax
import jax.numpy as jnp
from jax import lax
from jax.experimental import pallas as pl
from jax.experimental.pallas import tpu as pltpu

f32 = jnp.float32
bf16 = jnp.bfloat16
i32 = jnp.int32

NORM_EPS = 1e-6
CHUNK = 64
GDN_QK_HEADS = 4
GDN_V_HEADS = 8
GDN_HEAD = 128
GDN_KEY_DIM = GDN_QK_HEADS * GDN_HEAD
GDN_VAL_DIM = GDN_V_HEADS * GDN_HEAD
GDN_CONV = 4
RWKV_HEAD = 64
RWKV_LN_EPS = 64e-5
RWKV_GROUP = 2
TOP_K = 4
SWIGLU_ALPHA = 1.702
SWIGLU_LIMIT = 7.0
COL_Z = 2 * GDN_KEY_DIM + GDN_VAL_DIM
COL_B = COL_Z + GDN_VAL_DIM

LANES = 128
SUBLANES = 8
VMEM_LIMIT_BYTES = 60 * 1024 * 1024

EXPERT_ROWS = 1024
EXPERT_SUB_ROWS = 256
EXPERT_FF_TILE = 256
INV_BLOCK = 16


def _cparams(sem):
    return pltpu.CompilerParams(dimension_semantics=sem, vmem_limit_bytes=VMEM_LIMIT_BYTES)


def _dot(a, b):
    return jnp.dot(a.astype(bf16), b.astype(bf16), preferred_element_type=f32)


def _dot_nt(a, b):
    return lax.dot_general(a.astype(bf16), b.astype(bf16), (((1,), (1,)), ((), ())), preferred_element_type=f32)


def _dot_tn(a, b):
    return lax.dot_general(a.astype(bf16), b.astype(bf16), (((0,), (0,)), ((), ())), preferred_element_type=f32)


def _dot_f32(a, b):
    return jnp.dot(a, b, preferred_element_type=f32, precision=lax.Precision.HIGHEST)


def _softplus(z):
    return jnp.maximum(z, 0.0) + jnp.log1p(jnp.exp(-jnp.abs(z)))


def _iota(shape, axis):
    return lax.broadcasted_iota(i32, shape, axis)


def _pack_bf16_pairs(h):
    half = h.shape[1] // 2
    lo = lax.bitcast_convert_type(h[:, :half].astype(bf16).astype(f32), jnp.uint32) >> 16
    hi = lax.bitcast_convert_type(h[:, half:].astype(bf16).astype(f32), jnp.uint32) & jnp.uint32(0xFFFF0000)
    return lo | hi


def _unpack_bf16_pairs(u):
    lo = lax.bitcast_convert_type(u << 16, f32).astype(bf16)
    hi = lax.bitcast_convert_type(u & jnp.uint32(0xFFFF0000), f32).astype(bf16)
    return jnp.concatenate([lo, hi], axis=1)


def _ada_kernel(c_ref, w_ref, b_ref, o_ref):
    c = c_ref[...]
    s = c * jax.nn.sigmoid(c)
    o_ref[...] = _dot(s, w_ref[...]) + b_ref[...]


def _ada(c, w, b):
    B, D = c.shape
    N = w.shape[1]
    Bp = -(-B // SUBLANES) * SUBLANES
    cp = jnp.pad(c, ((0, Bp - B), (0, 0)))
    tn = 1024
    out = pl.pallas_call(
        _ada_kernel,
        grid=(N // tn,),
        in_specs=[pl.BlockSpec((Bp, D), lambda j: (0, 0)),
                  pl.BlockSpec((D, tn), lambda j: (0, j)),
                  pl.BlockSpec((1, tn), lambda j: (0, j))],
        out_specs=pl.BlockSpec((Bp, tn), lambda j: (0, j)),
        out_shape=jax.ShapeDtypeStruct((Bp, N), f32),
        compiler_params=_cparams(("arbitrary",)),
        name="ada",
    )(cp, w, b.reshape(1, N))
    return out[:B]


def _premix_kernel(x_ref, xp_ref, sc_ref, sh_ref, nw_ref, mix_ref, w1_ref, w2_ref, w0_ref, a1_ref, a2_ref, a0_ref,
                   g1_ref, g2_ref, h_ref, xr_ref, xk_ref, xv_ref, lw_ref, a_ref, gate_ref):
    i = pl.program_id(1)
    scale = nw_ref[...] * (1.0 + sc_ref[...])
    shift = sh_ref[...]

    def modulated_norm(x):
        ms = jnp.mean(x * x, axis=-1, keepdims=True)
        return x * lax.rsqrt(ms + NORM_EPS) * scale + shift

    h = modulated_norm(x_ref[...])
    hp = modulated_norm(xp_ref[...])[SUBLANES - 1:SUBLANES, :]
    hp = jnp.where(i == 0, 0.0, hp)
    rows = _iota(h.shape, 0)
    hprev = jnp.where(rows == 0, hp, pltpu.roll(h, 1, 0))
    dx = hprev - h
    mix = mix_ref[...]
    h_ref[...] = h.astype(bf16)
    xr_ref[...] = (h + dx * mix[0:1]).astype(bf16)
    xk_ref[...] = (h + dx * mix[2:3]).astype(bf16)
    xv_ref[...] = (h + dx * mix[3:4]).astype(bf16)
    xw = h + dx * mix[1:2]
    xa = h + dx * mix[4:5]
    xg = h + dx * mix[5:6]
    wl = w0_ref[...] + _dot(jnp.tanh(_dot(xw, w1_ref[...])), w2_ref[...])
    w_log = -_softplus(-wl) - 0.5
    lw_ref[...] = -jnp.exp(w_log)
    a_ref[...] = jax.nn.sigmoid(a0_ref[...] + _dot(_dot(xa, a1_ref[...]), a2_ref[...]))
    gate_ref[...] = _dot(jax.nn.sigmoid(_dot(xg, g1_ref[...])), g2_ref[...])


def _pad_to(a, axis, mult):
    n = a.shape[axis]
    npad = -(-n // mult) * mult - n
    if npad == 0:
        return a
    pads = [(0, 0)] * a.ndim
    pads[axis] = (0, npad)
    return jnp.pad(a, pads)


def _premix(x, sc, sh, nw, mix, w1, w2, w0, a1, a2, a0, g1, g2, tm=256):
    B, S, D = x.shape
    R = w2.shape[1]
    w1p = _pad_to(w1, 1, LANES).astype(bf16)
    w2p = _pad_to(w2, 0, LANES).astype(bf16)
    a1p = _pad_to(a1, 1, LANES).astype(bf16)
    a2p = _pad_to(a2, 0, LANES).astype(bf16)
    g1p = g1.astype(bf16)
    g2p = g2.astype(bf16)
    mixp = _pad_to(mix, 0, SUBLANES)
    nb = tm // SUBLANES

    def full(a):
        return pl.BlockSpec(a.shape, lambda b, i: (0,) * a.ndim)

    row = lambda w, dt: (pl.BlockSpec((None, tm, w), lambda b, i: (b, i, 0)), jax.ShapeDtypeStruct((B, S, w), dt))
    outs = [row(D, bf16), row(D, bf16), row(D, bf16), row(D, bf16), row(R, f32), row(R, f32), row(R, f32)]
    per_b = pl.BlockSpec((None, 1, D), lambda b, i: (b, 0, 0))
    args = (x, x, sc, sh, nw.reshape(1, D), mixp, w1p, w2p, w0.reshape(1, R), a1p, a2p, a0.reshape(1, R), g1p, g2p)
    in_specs = [pl.BlockSpec((None, tm, D), lambda b, i: (b, i, 0)),
                pl.BlockSpec((None, SUBLANES, D), lambda b, i: (b, jnp.maximum(i * nb - 1, 0), 0)),
                per_b, per_b] + [full(a) for a in args[4:]]
    return pl.pallas_call(
        _premix_kernel,
        grid=(B, S // tm),
        in_specs=in_specs,
        out_specs=[o[0] for o in outs],
        out_shape=[o[1] for o in outs],
        compiler_params=_cparams(("parallel", "arbitrary")),
        name="premix",
    )(*args)


def _mm_kernel(a_ref, b_ref, o_ref):
    o_ref[...] = jnp.dot(a_ref[...], b_ref[...], preferred_element_type=f32)


def _mm(a, b, tm=1024, tn=1024):
    M, K = a.shape
    N = b.shape[1]
    tm = min(tm, M)
    tn = min(tn, N)
    assert M % tm == 0 and N % tn == 0
    return pl.pallas_call(
        _mm_kernel,
        grid=(M // tm, N // tn),
        in_specs=[pl.BlockSpec((tm, K), lambda i, j: (i, 0)),
                  pl.BlockSpec((K, tn), lambda i, j: (0, j))],
        out_specs=pl.BlockSpec((tm, tn), lambda i, j: (i, j)),
        out_shape=jax.ShapeDtypeStruct((M, N), f32),
        compiler_params=_cparams(("parallel", "arbitrary")),
        name="mm",
    )(a, b)


def _gdn_prep_kernel(x_ref, xp_ref, ba_ref, cw_ref, alog_ref, dtb_ref, q_ref, k_ref, v_ref, bg_ref):
    i = pl.program_id(1)
    x = x_ref[...]
    tm = x.shape[0]
    xp = jnp.where(i == 0, 0.0, xp_ref[...])
    cw = cw_ref[...]
    rows8 = _iota(xp.shape, 0)
    acc = x * cw[GDN_CONV - 1:GDN_CONV]
    for j in range(1, GDN_CONV):
        rolled = pltpu.roll(x, j, 0)
        head = jnp.where(rows8 < j, pltpu.roll(xp, j, 0), rolled[0:SUBLANES])
        shifted = jnp.concatenate([head, rolled[SUBLANES:]], axis=0)
        acc = acc + shifted * cw[GDN_CONV - 1 - j:GDN_CONV - j]
    y = acc * jax.nn.sigmoid(acc)

    def l2n(t):
        return t * lax.rsqrt(jnp.sum(t * t, axis=-1, keepdims=True) + 1e-6)

    for hq in range(GDN_QK_HEADS):
        sl = slice(hq * GDN_HEAD, (hq + 1) * GDN_HEAD)
        q_ref[:, sl] = l2n(y[:, sl]) * (GDN_HEAD ** -0.5)
        ks = slice(GDN_KEY_DIM + hq * GDN_HEAD, GDN_KEY_DIM + (hq + 1) * GDN_HEAD)
        k_ref[:, sl] = l2n(y[:, ks])
    v_ref[...] = y[:, 2 * GDN_KEY_DIM:]

    ba = ba_ref[...]
    lane = _iota(ba.shape, 1)
    beta = jax.nn.sigmoid(ba)
    gstep = -jnp.exp(alog_ref[...]) * _softplus(ba + dtb_ref[...])
    gstep = jnp.where((lane >= GDN_V_HEADS) & (lane < 2 * GDN_V_HEADS), gstep, 0.0)
    r = _iota((tm, tm), 0)
    c = _iota((tm, tm), 1)
    tri = jnp.where((c <= r) & (r // CHUNK == c // CHUNK), 1.0, 0.0).astype(f32)
    gcum = _dot_f32(tri, gstep)
    bg_ref[...] = jnp.where(lane < GDN_V_HEADS, beta, gcum)


def _gdn_prep(qkv, ba, conv_w, a_log, dt_bias, tm=256):
    B, S, _ = qkv.shape
    C2 = 2 * GDN_KEY_DIM + GDN_VAL_DIM
    cwp = _pad_to(conv_w, 0, SUBLANES)
    alog = jnp.zeros((1, LANES), f32).at[0, GDN_V_HEADS:2 * GDN_V_HEADS].set(a_log)
    dtb = jnp.zeros((1, LANES), f32).at[0, GDN_V_HEADS:2 * GDN_V_HEADS].set(dt_bias)
    nb = tm // SUBLANES
    row = lambda w: (pl.BlockSpec((None, tm, w), lambda b, i: (b, i, 0)), jax.ShapeDtypeStruct((B, S, w), f32))
    outs = [row(GDN_KEY_DIM), row(GDN_KEY_DIM), row(GDN_VAL_DIM), row(LANES)]
    return pl.pallas_call(
        _gdn_prep_kernel,
        grid=(B, S // tm),
        in_specs=[pl.BlockSpec((None, tm, C2), lambda b, i: (b, i, 0)),
                  pl.BlockSpec((None, SUBLANES, C2), lambda b, i: (b, jnp.maximum(i * nb - 1, 0), 0)),
                  pl.BlockSpec((None, tm, LANES), lambda b, i: (b, i, 0)),
                  pl.BlockSpec(cwp.shape, lambda b, i: (0, 0)),
                  pl.BlockSpec((1, LANES), lambda b, i: (0, 0)),
                  pl.BlockSpec((1, LANES), lambda b, i: (0, 0))],
        out_specs=[o[0] for o in outs],
        out_shape=[o[1] for o in outs],
        compiler_params=_cparams(("parallel", "arbitrary")),
        name="gdn_prep",
    )(qkv, qkv, ba, cwp, alog, dtb)


def _seqs_per_step(batch):
    return next(n for n in (4, 2, 1) if batch % n == 0)


def _cumsum_rows(tri, x):
    hi = x.astype(bf16)
    r1 = x - hi.astype(f32)
    mid = r1.astype(bf16)
    lo = (r1 - mid.astype(f32)).astype(bf16)
    return (jnp.dot(tri, hi, preferred_element_type=f32) + jnp.dot(tri, mid, preferred_element_type=f32)
            + jnp.dot(tri, lo, preferred_element_type=f32))


def _block_diag(y, nblk):
    C, W = y.shape
    w = W // nblk
    t = jnp.concatenate([y] * nblk, axis=0)
    r = _iota(t.shape, 0) // C
    c = _iota(t.shape, 1) // w
    return jnp.where(r == c, t, jnp.zeros_like(t))


def _cat_matmul(xcat, ycat, nblk):
    return _dot(xcat, _block_diag(ycat.astype(bf16), nblk))


def _unit_lower_inverse(lcats, nblk):
    C = lcats[0].shape[0]
    assert C // INV_BLOCK == 4
    t = _iota(lcats[0].shape, 0)
    j = _iota(lcats[0].shape, 1) % C
    eye = jnp.where(t == j, 1.0, 0.0).astype(f32)
    in_diag = (t // INV_BLOCK) == (j // INV_BLOCK)
    dparts = [jnp.where(in_diag, l, 0.0) for l in lcats]
    oparts = [l - d for l, d in zip(lcats, dparts)]
    ps = [-d for d in dparts]
    xs = [eye + p for p in ps]
    levels = INV_BLOCK.bit_length() - 2
    for _ in range(levels):
        ps = [_cat_matmul(p, p, nblk) for p in ps]
        xs = [x + _cat_matmul(x, p, nblk) for x, p in zip(xs, ps)]
    ns = [_cat_matmul(x, o, nblk) for x, o in zip(xs, oparts)]
    n2s = [_cat_matmul(n, n, nblk) for n in ns]
    n3s = [_cat_matmul(n, n2, nblk) for n, n2 in zip(ns, n2s)]
    return [_cat_matmul(eye - n + n2 - n3, x, nblk) for n, n2, n3, x in zip(ns, n2s, n3s, xs)]


def _gdn_chunk_kernel(q_ref, k_ref, v_ref, z_ref, bg_ref, grow_ref, nw_ref, o_ref, state_ref):
    c_id = pl.program_id(1)

    @pl.when(c_id == 0)
    def _():
        state_ref[...] = jnp.zeros_like(state_ref)

    C = CHUNK
    H = GDN_V_HEADS
    HQ = GDN_QK_HEADS
    HD = GDN_HEAD
    rep = H // HQ
    G = 2
    NG = H // G
    seqs = range(q_ref.shape[0])
    nw = nw_ref[...]
    t = _iota((C, G * C), 0)
    j = _iota((C, G * C), 1) % C
    incl = j <= t
    strict = j < t
    bg = [bg_ref[bb] for bb in seqs]
    grow_all = [grow_ref[bb] for bb in seqs]
    qk_units = [(bb, h) for bb in seqs for h in range(HQ)]
    q = {(bb, h): q_ref[bb, :, h * HD:(h + 1) * HD] for bb, h in qk_units}
    k = {(bb, h): k_ref[bb, :, h * HD:(h + 1) * HD] for bb, h in qk_units}
    kk = {u: _dot_nt(k[u], k[u]) for u in qk_units}
    qk = {u: _dot_nt(q[u], k[u]) for u in qk_units}
    lowers, intras = [], []
    for bb in seqs:
        for g in range(NG):
            heads = range(g * G, (g + 1) * G)
            gcol = jnp.concatenate([jnp.broadcast_to(bg[bb][:, H + h:H + h + 1], (C, C)) for h in heads], axis=1)
            bcol = jnp.concatenate([jnp.broadcast_to(bg[bb][:, h:h + 1], (C, C)) for h in heads], axis=1)
            grow = grow_all[bb][:, g * G * C:(g + 1) * G * C]
            decay = jnp.where(incl, jnp.exp(jnp.where(incl, gcol - grow, 0.0)), 0.0)
            kkcat = jnp.concatenate([kk[(bb, h // rep)] for h in heads], axis=1)
            qkcat = jnp.concatenate([qk[(bb, h // rep)] for h in heads], axis=1)
            lowers.append(jnp.where(strict, kkcat * bcol * decay, 0.0))
            intras.append(qkcat * decay)
    tinvs = _unit_lower_inverse(lowers, G)

    def head_block(mats, bb, h):
        return mats[bb * NG + h // G][:, (h % G) * C:(h % G + 1) * C]

    units = [(bb, h) for bb in seqs for h in range(H)]
    g_c = {(bb, h): bg[bb][:, H + h:H + h + 1] for bb, h in units}
    beta = {(bb, h): bg[bb][:, h:h + 1] for bb, h in units}
    g_last = {(bb, h): bg[bb][C - 1:C, H + h:H + h + 1] for bb, h in units}
    kh = {(bb, h): k[(bb, h // rep)] for bb, h in units}
    qh = {(bb, h): q[(bb, h // rep)] for bb, h in units}
    rhs = {(bb, h): jnp.concatenate([v_ref[bb, :, h * HD:(h + 1) * HD] * beta[(bb, h)],
                                     kh[(bb, h)] * beta[(bb, h)] * jnp.exp(g_c[(bb, h)])], axis=1) for bb, h in units}
    sol = {(bb, h): _dot(head_block(tinvs, bb, h), rhs[(bb, h)]) for bb, h in units}
    s = {(bb, h): state_ref[bb * H + h] for bb, h in units}
    v_new = {u: sol[u][:, :HD] - _dot(sol[u][:, HD:], s[u]) for u in units}
    o = {(bb, h): _dot(qh[(bb, h)] * jnp.exp(g_c[(bb, h)]), s[(bb, h)])
         + _dot(head_block(intras, bb, h), v_new[(bb, h)]) for bb, h in units}
    s_new = {u: s[u] * jnp.exp(g_last[u]) + _dot_tn(kh[u] * jnp.exp(g_last[u] - g_c[u]), v_new[u]) for u in units}
    for bb, h in units:
        u = (bb, h)
        state_ref[bb * H + h] = s_new[u]
        on = o[u] * lax.rsqrt(jnp.mean(o[u] * o[u], axis=-1, keepdims=True) + NORM_EPS) * nw
        zh = z_ref[bb, :, h * HD:(h + 1) * HD]
        o_ref[bb, :, h * HD:(h + 1) * HD] = (on * (zh * jax.nn.sigmoid(zh))).astype(o_ref.dtype)


def _gdn_chunk(q, k, v, qkvz, bg, grow, norm_w):
    B, S, _ = q.shape
    nc = S // CHUNK
    zblk = COL_Z // GDN_VAL_DIM
    bb = _seqs_per_step(B)
    return pl.pallas_call(
        _gdn_chunk_kernel,
        grid=(B // bb, nc),
        in_specs=[pl.BlockSpec((bb, CHUNK, GDN_KEY_DIM), lambda b, c: (b, c, 0)),
                  pl.BlockSpec((bb, CHUNK, GDN_KEY_DIM), lambda b, c: (b, c, 0)),
                  pl.BlockSpec((bb, CHUNK, GDN_VAL_DIM), lambda b, c: (b, c, 0)),
                  pl.BlockSpec((bb, CHUNK, GDN_VAL_DIM), lambda b, c: (b, c, zblk)),
                  pl.BlockSpec((bb, CHUNK, LANES), lambda b, c: (b, c, 0)),
                  pl.BlockSpec((bb, None, 1, GDN_V_HEADS * CHUNK), lambda b, c: (b, c, 0, 0)),
                  pl.BlockSpec((1, GDN_HEAD), lambda b, c: (0, 0))],
        out_specs=pl.BlockSpec((bb, CHUNK, GDN_VAL_DIM), lambda b, c: (b, c, 0)),
        out_shape=jax.ShapeDtypeStruct((B, S, GDN_VAL_DIM), bf16),
        scratch_shapes=[pltpu.VMEM((bb * GDN_V_HEADS, GDN_HEAD, GDN_HEAD), f32)],
        compiler_params=_cparams(("parallel", "arbitrary")),
        name="gdn_chunk",
    )(q, k, v, qkvz, bg, grow, norm_w.reshape(1, GDN_HEAD))


def _head_sum(x):
    parts = []
    for s in range(x.shape[1] // LANES):
        xs = x[:, s * LANES:(s + 1) * LANES]
        lo = _iota(xs.shape, 1) < RWKV_HEAD
        s0 = jnp.sum(jnp.where(lo, xs, 0.0), axis=-1, keepdims=True)
        s1 = jnp.sum(jnp.where(lo, 0.0, xs), axis=-1, keepdims=True)
        parts.append(jnp.where(lo, s0, s1))
    return jnp.concatenate(parts, axis=1)


def _rwkv_chunk_kernel(r_ref, k_ref, v_ref, lw_ref, a_ref, gate_ref, kk_ref, ka_ref, rk_ref, lnw_ref, lnb_ref,
                       o_ref, state_ref):
    c_id = pl.program_id(1)

    @pl.when(c_id == 0)
    def _():
        state_ref[...] = jnp.zeros_like(state_ref)

    C = CHUNK
    G = RWKV_GROUP
    W = G * RWKV_HEAD
    n_groups = r_ref.shape[2] // W
    units = [(bb, g) for bb in range(r_ref.shape[0]) for g in range(n_groups)]
    sl = {(bb, g): slice(g * W, (g + 1) * W) for bb, g in units}
    tr = _iota((C, C), 0)
    tc = _iota((C, C), 1)
    tri_incl = jnp.where(tc <= tr, 1.0, 0.0).astype(bf16)
    t = _iota((C, W), 0)
    j = _iota((C, W), 1) % C
    incl = j <= t
    strict = j < t
    same_head = (_iota((W, W), 0) // RWKV_HEAD) == (_iota((W, W), 1) // RWKV_HEAD)
    r = {u: r_ref[u[0], :, sl[u]] for u in units}
    kr = {u: k_ref[u[0], :, sl[u]] for u in units}
    v = {u: v_ref[u[0], :, sl[u]] for u in units}
    lw = {u: lw_ref[u[0], :, sl[u]] for u in units}
    a = {u: a_ref[u[0], :, sl[u]] for u in units}
    cw = {u: _cumsum_rows(tri_incl, lw[u]) for u in units}
    kkv = {u: kr[u] * kk_ref[:, sl[u]] for u in units}
    kk = {u: kkv[u] * lax.rsqrt(_head_sum(kkv[u] * kkv[u]) + 1e-6) for u in units}
    kmod = {u: kr[u] * (1.0 + (a[u] - 1.0) * ka_ref[:, sl[u]]) for u in units}
    bvec = {u: kk[u] * a[u] for u in units}
    cl = {u: cw[u][C - 1:C, :] for u in units}
    e_out = {u: jnp.exp(-cw[u]) for u in units}
    e_end = {u: jnp.exp(cl[u] - cw[u]) for u in units}
    ar = {u: jnp.concatenate([-kk[u] * jnp.exp(cw[u] - lw[u]), r[u] * jnp.exp(cw[u])], axis=0).astype(bf16)
          for u in units}
    fb = {u: _dot_nt(ar[u], _block_diag((bvec[u] * e_out[u]).astype(bf16), G)) for u in units}
    fk = {u: _dot_nt(ar[u], _block_diag((kmod[u] * e_out[u]).astype(bf16), G)) for u in units}
    s = {(bb, g): state_ref[bb * n_groups + g] for bb, g in units}
    ars = {u: _dot_nt(ar[u], s[u]) for u in units}
    vbd = {u: _block_diag(v[u].astype(bf16), G) for u in units}
    tinv = dict(zip(units, _unit_lower_inverse([jnp.where(strict, -fb[u][:C], 0.0) for u in units], G)))
    z = {u: ars[u][:C] + _dot(jnp.where(strict, fk[u][:C], 0.0), vbd[u]) for u in units}
    uu = {u: _cat_matmul(tinv[u], z[u], G) for u in units}
    y = {u: ars[u][C:] + _cat_matmul(jnp.where(incl, fb[u][C:], 0.0), uu[u], G)
         + _dot(jnp.where(incl, fk[u][C:], 0.0), vbd[u]) for u in units}
    s_new = {u: jnp.where(same_head,
                          s[u] * jnp.exp(cl[u]) + _dot_tn(jnp.concatenate([uu[u], v[u]], axis=0),
                                                          jnp.concatenate([bvec[u] * e_end[u], kmod[u] * e_end[u]],
                                                                          axis=0)), 0.0) for u in units}
    for bb, g in units:
        u = (bb, g)
        state_ref[bb * n_groups + g] = s_new[u]
        mu = _head_sum(y[u]) * (1.0 / RWKV_HEAD)
        d = y[u] - mu
        var = _head_sum(d * d) * (1.0 / RWKV_HEAD)
        yn = d * lax.rsqrt(var + RWKV_LN_EPS) * lnw_ref[:, sl[u]] + lnb_ref[:, sl[u]]
        bonus = _head_sum(r[u] * kmod[u] * rk_ref[:, sl[u]]) * v[u]
        o_ref[bb, :, sl[u]] = ((yn + bonus) * gate_ref[bb, :, sl[u]]).astype(o_ref.dtype)


def _rwkv_chunk(r, k, v, lw, a, gate, k_k, k_a, r_k, ln_w, ln_b):
    B, S, R = r.shape
    nc = S // CHUNK
    W = RWKV_GROUP * RWKV_HEAD
    bb = _seqs_per_step(B)
    row = pl.BlockSpec((bb, CHUNK, R), lambda b, c: (b, c, 0))
    par = pl.BlockSpec((1, R), lambda b, c: (0, 0))
    params = [p.reshape(1, R) for p in (k_k, k_a, r_k, ln_w, ln_b)]
    return pl.pallas_call(
        _rwkv_chunk_kernel,
        grid=(B // bb, nc),
        in_specs=[row] * 6 + [par] * 5,
        out_specs=row,
        out_shape=jax.ShapeDtypeStruct((B, S, R), bf16),
        scratch_shapes=[pltpu.VMEM((bb * (R // W), W, W), f32)],
        compiler_params=_cparams(("parallel", "arbitrary")),
        name="rwkv_chunk",
    )(r, k, v, lw, a, gate, *params)


def _merge_kernel(yg_ref, yr_ref, gl_ref, x_ref, gt_ref, nw_ref, wug_ref, wur_ref, wo_ref, o_ref):
    D = x_ref.shape[1]
    sg = jax.nn.sigmoid(gl_ref[...])
    merged = (sg[:, :D] * jnp.dot(yg_ref[...], wug_ref[...], preferred_element_type=f32)
              + sg[:, D:] * jnp.dot(yr_ref[...], wur_ref[...], preferred_element_type=f32))
    out = _dot(merged, wo_ref[...])
    normed = out * lax.rsqrt(jnp.mean(out * out, axis=-1, keepdims=True) + NORM_EPS) * nw_ref[...]
    o_ref[...] = x_ref[...] + gt_ref[...] * normed


def _merge(yg, yr, gates, x, gt, nw, wug, wur, wo, tm=256):
    B, S, D = x.shape
    full = lambda a: pl.BlockSpec(a.shape, lambda b, i: (0,) * a.ndim)
    row = lambda w: pl.BlockSpec((None, tm, w), lambda b, i: (b, i, 0))
    wug, wur, wo = wug.astype(bf16), wur.astype(bf16), wo.astype(bf16)
    nw = nw.reshape(1, D)
    return pl.pallas_call(
        _merge_kernel,
        grid=(B, S // tm),
        in_specs=[row(yg.shape[2]), row(yr.shape[2]), row(2 * D), row(D),
                  pl.BlockSpec((None, 1, D), lambda b, i: (b, 0, 0)), full(nw), full(wug), full(wur), full(wo)],
        out_specs=row(D),
        out_shape=jax.ShapeDtypeStruct((B, S, D), f32),
        compiler_params=_cparams(("parallel", "arbitrary")),
        name="merge",
    )(yg, yr, gates, x, gt, nw, wug, wur, wo)


def _router_kernel(x_ref, sc_ref, sh_ref, nw_ref, whi_ref, wlo_ref, rb_ref, h_ref, sel_ref, prob_ref, cnt_ref,
                   carry_ref, *, n_experts):
    step = pl.program_id(0)

    @pl.when(step == 0)
    def _():
        carry_ref[...] = jnp.zeros_like(carry_ref)

    x = x_ref[...]
    tm = x.shape[0]
    ms = jnp.mean(x * x, axis=-1, keepdims=True)
    h = x * lax.rsqrt(ms + NORM_EPS) * (nw_ref[...] * (1.0 + sc_ref[...])) + sh_ref[...]
    h_ref[...] = _pack_bf16_pairs(h)
    hi = h.astype(bf16)
    lo = (h - hi.astype(f32)).astype(bf16)
    whi = whi_ref[...]
    logits = (jnp.dot(hi, whi, preferred_element_type=f32) + jnp.dot(lo, whi, preferred_element_type=f32)
              + jnp.dot(hi, wlo_ref[...], preferred_element_type=f32)) + rb_ref[...]
    lane = _iota(logits.shape, 1)
    neg = jnp.float32(-jnp.inf)
    cur = jnp.where(lane < n_experts, logits, neg)
    vals, idxs = [], []
    for _ in range(TOP_K):
        m = jnp.max(cur, axis=-1, keepdims=True)
        ix = jnp.min(jnp.where(cur == m, lane, LANES), axis=-1, keepdims=True)
        vals.append(m)
        idxs.append(ix)
        cur = jnp.where(lane == ix, neg, cur)
    es = [jnp.exp(vv - vals[0]) for vv in vals]
    denom = es[0] + es[1] + es[2] + es[3]
    onehots = [lane == ix for ix in idxs]
    msel = jnp.zeros(logits.shape, f32)
    for oh in onehots:
        msel = msel + jnp.where(oh, 1.0, 0.0)
    r = _iota((tm, tm), 0)
    c = _iota((tm, tm), 1)
    tri = jnp.where(c < r, 1.0, 0.0).astype(bf16)
    carry = carry_ref[0:1, :]
    pref = jnp.dot(tri, msel.astype(bf16), preferred_element_type=f32) + carry
    sel = jnp.zeros(logits.shape, i32)
    prob = jnp.zeros(logits.shape, f32)
    for kk in range(TOP_K):
        rank = jnp.sum(jnp.where(onehots[kk], pref, 0.0), axis=-1, keepdims=True).astype(i32)
        sel = jnp.where(lane == kk, idxs[kk], sel)
        sel = jnp.where(lane == TOP_K + kk, rank, sel)
        prob = jnp.where(lane == kk, es[kk] / denom, prob)
    sel_ref[...] = sel
    prob_ref[...] = prob
    new_carry = carry + jnp.sum(msel, axis=0, keepdims=True)
    carry_ref[...] = jnp.broadcast_to(new_carry, carry_ref.shape)
    cnt_ref[...] = jnp.broadcast_to(new_carry, cnt_ref.shape)


def _router(x1, sc, sh, nw, rw, rb, tm=256):
    B, S, D = x1.shape
    E = rw.shape[1]
    rwp = _pad_to(rw, 1, LANES)
    whi = rwp.astype(bf16)
    wlo = (rwp - whi.astype(f32)).astype(bf16)
    rbp = _pad_to(rb.reshape(1, E), 1, LANES)
    spt = S // tm
    row = lambda w: pl.BlockSpec((None, tm, w), lambda i: (i // spt, i % spt, 0))
    per_b = pl.BlockSpec((None, 1, D), lambda i: (i // spt, 0, 0))
    full = lambda a: pl.BlockSpec(a.shape, lambda i: (0,) * a.ndim)
    nw = nw.reshape(1, D)
    return pl.pallas_call(
        functools.partial(_router_kernel, n_experts=E),
        grid=(B * spt,),
        in_specs=[row(D), per_b, per_b, full(nw), full(whi), full(wlo), full(rbp)],
        out_specs=[row(D // 2), row(LANES), row(LANES), pl.BlockSpec((SUBLANES, LANES), lambda i: (0, 0))],
        out_shape=[jax.ShapeDtypeStruct((B, S, D // 2), jnp.uint32), jax.ShapeDtypeStruct((B, S, LANES), i32),
                   jax.ShapeDtypeStruct((B, S, LANES), f32), jax.ShapeDtypeStruct((SUBLANES, LANES), f32)],
        scratch_shapes=[pltpu.VMEM((SUBLANES, LANES), f32)],
        compiler_params=_cparams(("arbitrary",)),
        name="router",
    )(x1, sc, sh, nw, whi, wlo, rbp)


def _dispatch_kernel(pos_ref, h_ref, init_ref, xs_ref, sem):
    del init_ref
    tm = h_ref.shape[0]

    def row_copy(r, kk):
        p = pos_ref[0, r * TOP_K + kk]
        return pltpu.make_async_copy(h_ref.at[pl.ds(r, 1), :], xs_ref.at[pl.ds(p, 1), :], sem)

    def issue(r, carry):
        for kk in range(TOP_K):
            row_copy(r, kk).start(priority=kk % 2)
        return carry

    def drain(r, carry):
        for kk in range(TOP_K):
            row_copy(r, kk).wait()
        return carry

    lax.fori_loop(0, tm, issue, 0)
    lax.fori_loop(0, tm, drain, 0)


def _dispatch(h2, pos, n_rows, tm=256):
    T, D = h2.shape
    pos3 = pos.reshape(T // tm, 1, tm * TOP_K)
    init = jnp.zeros((n_rows, D), h2.dtype)
    return pl.pallas_call(
        _dispatch_kernel,
        grid=(T // tm,),
        in_specs=[pl.BlockSpec((None, 1, tm * TOP_K), lambda i: (i, 0, 0), memory_space=pltpu.SMEM),
                  pl.BlockSpec((tm, D), lambda i: (i, 0)),
                  pl.BlockSpec(memory_space=pl.ANY)],
        out_specs=pl.BlockSpec(memory_space=pl.ANY),
        out_shape=jax.ShapeDtypeStruct((n_rows, D), h2.dtype),
        scratch_shapes=[pltpu.SemaphoreType.DMA],
        input_output_aliases={2: 0},
        compiler_params=_cparams(("arbitrary",)),
        name="dispatch",
    )(pos3, h2, init)


def _expert_kernel(be_ref, nv_ref, nu_ref, x_ref, wg_ref, wu_ref, bg_ref, bu_ref, wd_ref, bd_ref, o_ref, xb_ref,
                   acc_ref):
    del be_ref, nu_ref
    nb = pl.program_id(0)
    f = pl.program_id(1)
    n_valid = nv_ref[nb]

    @pl.when((n_valid > 0) & (f == 0))
    def _():
        xb_ref[...] = _unpack_bf16_pairs(x_ref[...])
        acc_ref[...] = jnp.broadcast_to(bd_ref[...], acc_ref.shape)

    @pl.when((n_valid == 0) & (f == 0))
    def _():
        o_ref[...] = jnp.zeros_like(o_ref)

    n_sub = (n_valid + EXPERT_SUB_ROWS - 1) // EXPERT_SUB_ROWS
    for ns in range(1, EXPERT_ROWS // EXPERT_SUB_ROWS + 1):
        rows = slice(0, ns * EXPERT_SUB_ROWS)

        @pl.when(n_sub == ns)
        def _():
            x = xb_ref[rows, :]
            g = jnp.dot(x, wg_ref[...].astype(bf16), preferred_element_type=f32) + bg_ref[...]
            u = jnp.dot(x, wu_ref[...].astype(bf16), preferred_element_type=f32) + bu_ref[...]
            gl = jnp.minimum(g, SWIGLU_LIMIT)
            up = jnp.clip(u, -SWIGLU_LIMIT, SWIGLU_LIMIT)
            act = gl * jax.nn.sigmoid(SWIGLU_ALPHA * gl) * (up + 1.0)
            acc_ref[rows, :] += jnp.dot(act.astype(bf16), wd_ref[...].astype(bf16), preferred_element_type=f32)

    @pl.when((n_valid > 0) & (f == pl.num_programs(1) - 1))
    def _():
        o_ref[...] = _pack_bf16_pairs(acc_ref[...])


def _experts(xs, block_e, n_valid, n_used, w_gu, b_gu, w_down, b_down):
    P, Dh = xs.shape
    D = 2 * Dh
    E, _, F2 = w_gu.shape
    F = F2 // 2
    tf = EXPERT_FF_TILE
    nf = F // tf
    NB = P // EXPERT_ROWS
    b_gu3 = b_gu.reshape(E, 1, F2)
    b_dn3 = b_down.reshape(E, 1, D)

    def blk(nb, nu):
        return jnp.minimum(nb, nu[0] - 1)

    def fidx(nb, f, nu):
        return jnp.where(nb < nu[0], f, nf - 1)

    grid_spec = pltpu.PrefetchScalarGridSpec(
        num_scalar_prefetch=3,
        grid=(NB, nf),
        in_specs=[
            pl.BlockSpec((EXPERT_ROWS, Dh), lambda nb, f, be, nv, nu: (blk(nb, nu), 0)),
            pl.BlockSpec((None, D, tf), lambda nb, f, be, nv, nu: (be[blk(nb, nu)], 0, fidx(nb, f, nu))),
            pl.BlockSpec((None, D, tf), lambda nb, f, be, nv, nu: (be[blk(nb, nu)], 0, nf + fidx(nb, f, nu))),
            pl.BlockSpec((None, 1, tf), lambda nb, f, be, nv, nu: (be[blk(nb, nu)], 0, fidx(nb, f, nu))),
            pl.BlockSpec((None, 1, tf), lambda nb, f, be, nv, nu: (be[blk(nb, nu)], 0, nf + fidx(nb, f, nu))),
            pl.BlockSpec((None, tf, D), lambda nb, f, be, nv, nu: (be[blk(nb, nu)], fidx(nb, f, nu), 0)),
            pl.BlockSpec((None, 1, D), lambda nb, f, be, nv, nu: (be[blk(nb, nu)], 0, 0)),
        ],
        out_specs=pl.BlockSpec((EXPERT_ROWS, Dh), lambda nb, f, be, nv, nu: (nb, 0)),
        scratch_shapes=[pltpu.VMEM((EXPERT_ROWS, D), bf16), pltpu.VMEM((EXPERT_ROWS, D), f32)],
    )
    return pl.pallas_call(
        _expert_kernel,
        grid_spec=grid_spec,
        out_shape=jax.ShapeDtypeStruct((P, Dh), jnp.uint32),
        compiler_params=_cparams(("arbitrary", "arbitrary")),
        name="experts",
    )(block_e, n_valid, n_used, xs, w_gu, w_gu, b_gu3, b_gu3, w_down, b_dn3)


def _combine_kernel(pos_ref, yb_ref, prob_ref, x_ref, gt_ref, nw_ref, o_ref, buf_ref, sem):
    tm = x_ref.shape[0]

    def row_copy(r, kk):
        p = pos_ref[0, r * TOP_K + kk]
        return pltpu.make_async_copy(yb_ref.at[pl.ds(p, 1), :], buf_ref.at[kk, pl.ds(r, 1), :], sem)

    def issue(r, carry):
        for kk in range(TOP_K):
            row_copy(r, kk).start(priority=kk % 2)
        return carry

    def drain(r, carry):
        for kk in range(TOP_K):
            row_copy(r, kk).wait()
        return carry

    lax.fori_loop(0, tm, issue, 0)
    lax.fori_loop(0, tm, drain, 0)
    prob = prob_ref[...]
    y_lo = None
    for kk in range(TOP_K):
        u = buf_ref[kk]
        lo = lax.bitcast_convert_type(u << 16, f32) * prob[:, kk:kk + 1]
        hi = lax.bitcast_convert_type(u & jnp.uint32(0xFFFF0000), f32) * prob[:, kk:kk + 1]
        y_lo, y_hi = (lo, hi) if y_lo is None else (y_lo + lo, y_hi + hi)
    y = jnp.concatenate([y_lo, y_hi], axis=1)
    normed = y * lax.rsqrt(jnp.mean(y * y, axis=-1, keepdims=True) + NORM_EPS) * nw_ref[...]
    o_ref[...] = x_ref[...] + gt_ref[...] * normed


def _combine(yb, pos, prob, x1, gt, nw, tm=256):
    B, S, D = x1.shape
    T = B * S
    spt = S // tm
    pos3 = pos.reshape(T // tm, 1, tm * TOP_K)
    row = lambda w: pl.BlockSpec((None, tm, w), lambda i: (i // spt, i % spt, 0))
    nw = nw.reshape(1, D)
    return pl.pallas_call(
        _combine_kernel,
        grid=(T // tm,),
        in_specs=[pl.BlockSpec((None, 1, tm * TOP_K), lambda i: (i, 0, 0), memory_space=pltpu.SMEM),
                  pl.BlockSpec(memory_space=pl.ANY),
                  row(LANES), row(D),
                  pl.BlockSpec((None, 1, D), lambda i: (i // spt, 0, 0)),
                  pl.BlockSpec(nw.shape, lambda i: (0, 0))],
        out_specs=row(D),
        out_shape=jax.ShapeDtypeStruct((B, S, D), f32),
        scratch_shapes=[pltpu.VMEM((TOP_K, tm, D // 2), jnp.uint32), pltpu.SemaphoreType.DMA],
        compiler_params=_cparams(("arbitrary",)),
        name="combine",
    )(pos3, yb, prob, x1, gt, nw)


def _mixer(x, sc1, sh1, gt1, norm_pre, norm_post, w_in, gdn_conv, gdn_A_log, gdn_dt_bias, gdn_norm, rwkv_mix,
           rwkv_w0, rwkv_w1, rwkv_w2, rwkv_a0, rwkv_a1, rwkv_a2, rwkv_g1, rwkv_g2, rwkv_k_k, rwkv_k_a, rwkv_r_k,
           rwkv_ln_w, rwkv_ln_b, w_up_gdn, w_up_rwkv, w_out):
    B, S, D = x.shape
    T = B * S
    R = rwkv_w2.shape[1]
    col_a = COL_B + GDN_V_HEADS
    col_gate = col_a + GDN_V_HEADS
    col_r = col_gate + 2 * D
    h, xr, xk, xv, lw, a_sig, gate = _premix(x, sc1, sh1, norm_pre, rwkv_mix, rwkv_w1, rwkv_w2, rwkv_w0, rwkv_a1,
                                             rwkv_a2, rwkv_a0, rwkv_g1, rwkv_g2)
    wb = w_in.astype(bf16)
    h2d = h.reshape(T, D)
    qkvz = _mm(h2d, wb[:, :COL_B]).reshape(B, S, COL_B)
    ba = _mm(h2d, _pad_to(wb[:, COL_B:col_gate], 1, LANES)).reshape(B, S, LANES)
    gates = _mm(h2d, wb[:, col_gate:col_r]).reshape(B, S, 2 * D)
    r = _mm(xr.reshape(T, D), wb[:, col_r:col_r + R]).reshape(B, S, R)
    k_r = _mm(xk.reshape(T, D), wb[:, col_r + R:col_r + 2 * R]).reshape(B, S, R)
    v_r = _mm(xv.reshape(T, D), wb[:, col_r + 2 * R:col_r + 3 * R]).reshape(B, S, R)

    q, k, v, bg = _gdn_prep(qkvz, ba, gdn_conv, gdn_A_log, gdn_dt_bias)
    nc = S // CHUNK
    grow = bg[:, :, GDN_V_HEADS:2 * GDN_V_HEADS].reshape(B, nc, CHUNK, GDN_V_HEADS)
    grow = grow.transpose(0, 1, 3, 2).reshape(B, nc, 1, GDN_V_HEADS * CHUNK)
    y_gdn = _gdn_chunk(q, k, v, qkvz, bg, grow, gdn_norm)
    y_rwkv = _rwkv_chunk(r, k_r, v_r, lw, a_sig, gate, rwkv_k_k, rwkv_k_a, rwkv_r_k.reshape(-1), rwkv_ln_w,
                         rwkv_ln_b)
    return _merge(y_gdn, y_rwkv, gates, x, gt1, norm_post, w_up_gdn, w_up_rwkv, w_out)


def _moe(x1, sc2, sh2, gt2, norm_pre, norm_post, router_w, router_b, w_gu, b_gu, w_down, b_down):
    B, S, D = x1.shape
    T = B * S
    E = router_w.shape[1]
    h2, sel, prob, cnt = _router(x1, sc2, sh2, norm_pre, router_w, router_b)
    sel = sel.reshape(T, LANES)
    e_idx = sel[:, :TOP_K]
    rank = sel[:, TOP_K:2 * TOP_K]
    counts = cnt[0, :E].astype(i32)
    padded = (counts + EXPERT_ROWS - 1) // EXPERT_ROWS * EXPERT_ROWS
    pad_end = jnp.cumsum(padded)
    pad_start = pad_end - padded
    pos = (pad_start[e_idx] + rank).astype(i32)
    P = T * TOP_K + E * EXPERT_ROWS
    NB = P // EXPERT_ROWS
    block_start = jnp.arange(NB, dtype=i32) * EXPERT_ROWS
    block_e = jnp.minimum(jnp.sum(pad_end[None, :] <= block_start[:, None], axis=1), E - 1).astype(i32)
    n_valid = jnp.clip(pad_start[block_e] + counts[block_e] - block_start, 0, EXPERT_ROWS).astype(i32)
    n_used = jnp.maximum(pad_end[-1:] // EXPERT_ROWS, 1).astype(i32)
    xs = _dispatch(h2.reshape(T, D // 2), pos, P)
    yb = _experts(xs, block_e, n_valid, n_used, w_gu, b_gu, w_down, b_down)
    return _combine(yb, pos, prob, x1, gt2, norm_post)


def kernel(x, c, ada_w, ada_b, norm_mix_pre, norm_mix_post, norm_ffn_pre, norm_ffn_post, w_in, gdn_conv, gdn_A_log,
           gdn_dt_bias, gdn_norm, rwkv_mix, rwkv_w0, rwkv_w1, rwkv_w2, rwkv_a0, rwkv_a1, rwkv_a2, rwkv_g1, rwkv_g2,
           rwkv_k_k, rwkv_k_a, rwkv_r_k, rwkv_ln_w, rwkv_ln_b, w_up_gdn, w_up_rwkv, w_out, router_w, router_b,
           exp_w_gu, exp_b_gu, exp_w_down, exp_b_down):
    B, S, D = x.shape
    for l in range(ada_w.shape[0]):
        mod = _ada(c, ada_w[l], ada_b[l])
        sh1, sc1, gt1, sh2, sc2, gt2 = [m.reshape(B, 1, D) for m in jnp.split(mod, 6, axis=-1)]
        x = _mixer(x, sc1, sh1, gt1, norm_mix_pre[l], norm_mix_post[l], w_in[l], gdn_conv[l], gdn_A_log[l],
                   gdn_dt_bias[l], gdn_norm[l], rwkv_mix[l], rwkv_w0[l], rwkv_w1[l], rwkv_w2[l], rwkv_a0[l],
                   rwkv_a1[l], rwkv_a2[l], rwkv_g1[l], rwkv_g2[l], rwkv_k_k[l], rwkv_k_a[l], rwkv_r_k[l],
                   rwkv_ln_w[l], rwkv_ln_b[l], w_up_gdn[l], w_up_rwkv[l], w_out[l])
        x = _moe(x, sc2, sh2, gt2, norm_ffn_pre[l], norm_ffn_post[l], router_w[l], router_b[l], exp_w_gu[l],
                 exp_b_gu[l], exp_w_down[l], exp_b_down[l])
    return x
```

```python
import functools

import jax
import jax.numpy as jnp
from jax import lax
from jax.experimental import pallas as pl
from jax.experimental.pallas import tpu as pltpu

f32 = jnp.float32
bf16 = jnp.bfloat16
i32 = jnp.int32

NORM_EPS = 1e-6
CHUNK = 64
GDN_QK_HEADS = 4
GDN_V_HEADS = 8
GDN_HEAD = 128
GDN_KEY_DIM = GDN_QK_HEADS * GDN_HEAD
GDN_VAL_DIM = GDN_V_HEADS * GDN_HEAD
GDN_CONV = 4
RWKV_HEAD = 64
RWKV_LN_EPS = 64e-5
RWKV_GROUP = 2
TOP_K = 4
SWIGLU_ALPHA = 1.702
SWIGLU_LIMIT = 7.0
COL_Z = 2 * GDN_KEY_DIM + GDN_VAL_DIM
COL_B = COL_Z + GDN_VAL_DIM

LANES = 128
SUBLANES = 8
VMEM_LIMIT_BYTES = 60 * 1024 * 1024

EXPERT_ROWS = 1024
EXPERT_SUB_ROWS = 256
EXPERT_FF_TILE = 256
INV_BLOCK = 16


def _cparams(sem):
    return pltpu.CompilerParams(dimension_semantics=sem, vmem_limit_bytes=VMEM_LIMIT_BYTES)


def _dot(a, b):
    return jnp.dot(a.astype(bf16), b.astype(bf16), preferred_element_type=f32)


def _dot_nt(a, b):
    return lax.dot_general(a.astype(bf16), b.astype(bf16), (((1,), (1,)), ((), ())), preferred_element_type=f32)


def _dot_tn(a, b):
    return lax.dot_general(a.astype(bf16), b.astype(bf16), (((0,), (0,)), ((), ())), preferred_element_type=f32)


def _dot_f32(a, b):
    return jnp.dot(a, b, preferred_element_type=f32, precision=lax.Precision.HIGHEST)


def _softplus(z):
    return jnp.maximum(z, 0.0) + jnp.log1p(jnp.exp(-jnp.abs(z)))


def _iota(shape, axis):
    return lax.broadcasted_iota(i32, shape, axis)


def _pack_bf16_pairs(h):
    half = h.shape[1] // 2
    lo = lax.bitcast_convert_type(h[:, :half].astype(bf16).astype(f32), jnp.uint32) >> 16
    hi = lax.bitcast_convert_type(h[:, half:].astype(bf16).astype(f32), jnp.uint32) & jnp.uint32(0xFFFF0000)
    return lo | hi


def _unpack_bf16_pairs(u):
    lo = lax.bitcast_convert_type(u << 16, f32).astype(bf16)
    hi = lax.bitcast_convert_type(u & jnp.uint32(0xFFFF0000), f32).astype(bf16)
    return jnp.concatenate([lo, hi], axis=1)


def _ada_kernel(c_ref, w_ref, b_ref, o_ref):
    c = c_ref[...]
    s = c * jax.nn.sigmoid(c)
    o_ref[...] = _dot(s, w_ref[...]) + b_ref[...]


def _ada(c, w, b):
    B, D = c.shape
    N = w.shape[1]
    Bp = -(-B // SUBLANES) * SUBLANES
    cp = jnp.pad(c, ((0, Bp - B), (0, 0)))
    tn = 1024
    out = pl.pallas_call(
        _ada_kernel,
        grid=(N // tn,),
        in_specs=[pl.BlockSpec((Bp, D), lambda j: (0, 0)),
                  pl.BlockSpec((D, tn), lambda j: (0, j)),
                  pl.BlockSpec((1, tn), lambda j: (0, j))],
        out_specs=pl.BlockSpec((Bp, tn), lambda j: (0, j)),
        out_shape=jax.ShapeDtypeStruct((Bp, N), f32),
        compiler_params=_cparams(("arbitrary",)),
        name="ada",
    )(cp, w, b.reshape(1, N))
    return out[:B]


def _premix_kernel(x_ref, xp_ref, sc_ref, sh_ref, nw_ref, mix_ref, w1_ref, w2_ref, w0_ref, a1_ref, a2_ref, a0_ref,
                   g1_ref, g2_ref, h_ref, xr_ref, xk_ref, xv_ref, lw_ref, a_ref, gate_ref):
    i = pl.program_id(1)
    scale = nw_ref[...] * (1.0 + sc_ref[...])
    shift = sh_ref[...]

    def modulated_norm(x):
        ms = jnp.mean(x * x, axis=-1, keepdims=True)
        return x * lax.rsqrt(ms + NORM_EPS) * scale + shift

    h = modulated_norm(x_ref[...])
    hp = modulated_norm(xp_ref[...])[SUBLANES - 1:SUBLANES, :]
    hp = jnp.where(i == 0, 0.0, hp)
    rows = _iota(h.shape, 0)
    hprev = jnp.where(rows == 0, hp, pltpu.roll(h, 1, 0))
    dx = hprev - h
    mix = mix_ref[...]
    h_ref[...] = h.astype(bf16)
    xr_ref[...] = (h + dx * mix[0:1]).astype(bf16)
    xk_ref[...] = (h + dx * mix[2:3]).astype(bf16)
    xv_ref[...] = (h + dx * mix[3:4]).astype(bf16)
    xw = h + dx * mix[1:2]
    xa = h + dx * mix[4:5]
    xg = h + dx * mix[5:6]
    wl = w0_ref[...] + _dot(jnp.tanh(_dot(xw, w1_ref[...])), w2_ref[...])
    w_log = -_softplus(-wl) - 0.5
    lw_ref[...] = -jnp.exp(w_log)
    a_ref[...] = jax.nn.sigmoid(a0_ref[...] + _dot(_dot(xa, a1_ref[...]), a2_ref[...]))
    gate_ref[...] = _dot(jax.nn.sigmoid(_dot(xg, g1_ref[...])), g2_ref[...])


def _pad_to(a, axis, mult):
    n = a.shape[axis]
    npad = -(-n // mult) * mult - n
    if npad == 0:
        return a
    pads = [(0, 0)] * a.ndim
    pads[axis] = (0, npad)
    return jnp.pad(a, pads)


def _premix(x, sc, sh, nw, mix, w1, w2, w0, a1, a2, a0, g1, g2, tm=256):
    B, S, D = x.shape
    R = w2.shape[1]
    w1p = _pad_to(w1, 1, LANES).astype(bf16)
    w2p = _pad_to(w2, 0, LANES).astype(bf16)
    a1p = _pad_to(a1, 1, LANES).astype(bf16)
    a2p = _pad_to(a2, 0, LANES).astype(bf16)
    g1p = g1.astype(bf16)
    g2p = g2.astype(bf16)
    mixp = _pad_to(mix, 0, SUBLANES)
    nb = tm // SUBLANES

    def full(a):
        return pl.BlockSpec(a.shape, lambda b, i: (0,) * a.ndim)

    row = lambda w, dt: (pl.BlockSpec((None, tm, w), lambda b, i: (b, i, 0)), jax.ShapeDtypeStruct((B, S, w), dt))
    outs = [row(D, bf16), row(D, bf16), row(D, bf16), row(D, bf16), row(R, f32), row(R, f32), row(R, f32)]
    per_b = pl.BlockSpec((None, 1, D), lambda b, i: (b, 0, 0))
    args = (x, x, sc, sh, nw.reshape(1, D), mixp, w1p, w2p, w0.reshape(1, R), a1p, a2p, a0.reshape(1, R), g1p, g2p)
    in_specs = [pl.BlockSpec((None, tm, D), lambda b, i: (b, i, 0)),
                pl.BlockSpec((None, SUBLANES, D), lambda b, i: (b, jnp.maximum(i * nb - 1, 0), 0)),
                per_b, per_b] + [full(a) for a in args[4:]]
    return pl.pallas_call(
        _premix_kernel,
        grid=(B, S // tm),
        in_specs=in_specs,
        out_specs=[o[0] for o in outs],
        out_shape=[o[1] for o in outs],
        compiler_params=_cparams(("parallel", "arbitrary")),
        name="premix",
    )(*args)


def _mm_kernel(a_ref, b_ref, o_ref):
    o_ref[...] = jnp.dot(a_ref[...], b_ref[...], preferred_element_type=f32)


def _mm(a, b, tm=1024, tn=1024):
    M, K = a.shape
    N = b.shape[1]
    tm = min(tm, M)
    tn = min(tn, N)
    assert M % tm == 0 and N % tn == 0
    return pl.pallas_call(
        _mm_kernel,
        grid=(M // tm, N // tn),
        in_specs=[pl.BlockSpec((tm, K), lambda i, j: (i, 0)),
                  pl.BlockSpec((K, tn), lambda i, j: (0, j))],
        out_specs=pl.BlockSpec((tm, tn), lambda i, j: (i, j)),
        out_shape=jax.ShapeDtypeStruct((M, N), f32),
        compiler_params=_cparams(("parallel", "arbitrary")),
        name="mm",
    )(a, b)


def _gdn_prep_kernel(x_ref, xp_ref, ba_ref, cw_ref, alog_ref, dtb_ref, q_ref, k_ref, v_ref, bg_ref):
    i = pl.program_id(1)
    x = x_ref[...]
    tm = x.shape[0]
    xp = jnp.where(i == 0, 0.0, xp_ref[...])
    cw = cw_ref[...]
    rows8 = _iota(xp.shape, 0)
    acc = x * cw[GDN_CONV - 1:GDN_CONV]
    for j in range(1, GDN_CONV):
        rolled = pltpu.roll(x, j, 0)
        head = jnp.where(rows8 < j, pltpu.roll(xp, j, 0), rolled[0:SUBLANES])
        shifted = jnp.concatenate([head, rolled[SUBLANES:]], axis=0)
        acc = acc + shifted * cw[GDN_CONV - 1 - j:GDN_CONV - j]
    y = acc * jax.nn.sigmoid(acc)

    def l2n(t):
        return t * lax.rsqrt(jnp.sum(t * t, axis=-1, keepdims=True) + 1e-6)

    for hq in range(GDN_QK_HEADS):
        sl = slice(hq * GDN_HEAD, (hq + 1) * GDN_HEAD)
        q_ref[:, sl] = l2n(y[:, sl]) * (GDN_HEAD ** -0.5)
        ks = slice(GDN_KEY_DIM + hq * GDN_HEAD, GDN_KEY_DIM + (hq + 1) * GDN_HEAD)
        k_ref[:, sl] = l2n(y[:, ks])
    v_ref[...] = y[:, 2 * GDN_KEY_DIM:]

    ba = ba_ref[...]
    lane = _iota(ba.shape, 1)
    beta = jax.nn.sigmoid(ba)
    gstep = -jnp.exp(alog_ref[...]) * _softplus(ba + dtb_ref[...])
    gstep = jnp.where((lane >= GDN_V_HEADS) & (lane < 2 * GDN_V_HEADS), gstep, 0.0)
    r = _iota((tm, tm), 0)
    c = _iota((tm, tm), 1)
    tri = jnp.where((c <= r) & (r // CHUNK == c // CHUNK), 1.0, 0.0).astype(f32)
    gcum = _dot_f32(tri, gstep)
    bg_ref[...] = jnp.where(lane < GDN_V_HEADS, beta, gcum)


def _gdn_prep(qkv, ba, conv_w, a_log, dt_bias, tm=256):
    B, S, _ = qkv.shape
    C2 = 2 * GDN_KEY_DIM + GDN_VAL_DIM
    cwp = _pad_to(conv_w, 0, SUBLANES)
    alog = jnp.zeros((1, LANES), f32).at[0, GDN_V_HEADS:2 * GDN_V_HEADS].set(a_log)
    dtb = jnp.zeros((1, LANES), f32).at[0, GDN_V_HEADS:2 * GDN_V_HEADS].set(dt_bias)
    nb = tm // SUBLANES
    row = lambda w: (pl.BlockSpec((None, tm, w), lambda b, i: (b, i, 0)), jax.ShapeDtypeStruct((B, S, w), f32))
    outs = [row(GDN_KEY_DIM), row(GDN_KEY_DIM), row(GDN_VAL_DIM), row(LANES)]
    return pl.pallas_call(
        _gdn_prep_kernel,
        grid=(B, S // tm),
        in_specs=[pl.BlockSpec((None, tm, C2), lambda b, i: (b, i, 0)),
                  pl.BlockSpec((None, SUBLANES, C2), lambda b, i: (b, jnp.maximum(i * nb - 1, 0), 0)),
                  pl.BlockSpec((None, tm, LANES), lambda b, i: (b, i, 0)),
                  pl.BlockSpec(cwp.shape, lambda b, i: (0, 0)),
                  pl.BlockSpec((1, LANES), lambda b, i: (0, 0)),
                  pl.BlockSpec((1, LANES), lambda b, i: (0, 0))],
        out_specs=[o[0] for o in outs],
        out_shape=[o[1] for o in outs],
        compiler_params=_cparams(("parallel", "arbitrary")),
        name="gdn_prep",
    )(qkv, qkv, ba, cwp, alog, dtb)


def _seqs_per_step(batch):
    return next(n for n in (4, 2, 1) if batch % n == 0)


def _cumsum_rows(tri, x):
    hi = x.astype(bf16)
    r1 = x - hi.astype(f32)
    mid = r1.astype(bf16)
    lo = (r1 - mid.astype(f32)).astype(bf16)
    return (jnp.dot(tri, hi, preferred_element_type=f32) + jnp.dot(tri, mid, preferred_element_type=f32)
            + jnp.dot(tri, lo, preferred_element_type=f32))


def _block_diag(y, nblk):
    C, W = y.shape
    w = W // nblk
    t = jnp.concatenate([y] * nblk, axis=0)
    r = _iota(t.shape, 0) // C
    c = _iota(t.shape, 1) // w
    return jnp.where(r == c, t, jnp.zeros_like(t))


def _cat_matmul(xcat, ycat, nblk):
    return _dot(xcat, _block_diag(ycat.astype(bf16), nblk))


def _unit_lower_inverse(lcats, nblk):
    C = lcats[0].shape[0]
    assert C // INV_BLOCK == 4
    t = _iota(lcats[0].shape, 0)
    j = _iota(lcats[0].shape, 1) % C
    eye = jnp.where(t == j, 1.0, 0.0).astype(f32)
    in_diag = (t // INV_BLOCK) == (j // INV_BLOCK)
    dparts = [jnp.where(in_diag, l, 0.0) for l in lcats]
    oparts = [l - d for l, d in zip(lcats, dparts)]
    ps = [-d for d in dparts]
    xs = [eye + p for p in ps]
    levels = INV_BLOCK.bit_length() - 2
    for _ in range(levels):
        ps = [_cat_matmul(p, p, nblk) for p in ps]
        xs = [x + _cat_matmul(x, p, nblk) for x, p in zip(xs, ps)]
    ns = [_cat_matmul(x, o, nblk) for x, o in zip(xs, oparts)]
    n2s = [_cat_matmul(n, n, nblk) for n in ns]
    n3s = [_cat_matmul(n, n2, nblk) for n, n2 in zip(ns, n2s)]
    return [_cat_matmul(eye - n + n2 - n3, x, nblk) for n, n2, n3, x in zip(ns, n2s, n3s, xs)]


def _gdn_chunk_kernel(q_ref, k_ref, v_ref, z_ref, bg_ref, grow_ref, nw_ref, o_ref, state_ref):
    c_id = pl.program_id(1)

    @pl.when(c_id == 0)
    def _():
        state_ref[...] = jnp.zeros_like(state_ref)

    C = CHUNK
    H = GDN_V_HEADS
    HQ = GDN_QK_HEADS
    HD = GDN_HEAD
    rep = H // HQ
    G = 2
    NG = H // G
    seqs = range(q_ref.shape[0])
    nw = nw_ref[...]
    t = _iota((C, G * C), 0)
    j = _iota((C, G * C), 1) % C
    incl = j <= t
    strict = j < t
    bg = [bg_ref[bb] for bb in seqs]
    grow_all = [grow_ref[bb] for bb in seqs]
    qk_units = [(bb, h) for bb in seqs for h in range(HQ)]
    q = {(bb, h): q_ref[bb, :, h * HD:(h + 1) * HD] for bb, h in qk_units}
    k = {(bb, h): k_ref[bb, :, h * HD:(h + 1) * HD] for bb, h in qk_units}
    kk = {u: _dot_nt(k[u], k[u]) for u in qk_units}
    qk = {u: _dot_nt(q[u], k[u]) for u in qk_units}
    lowers, intras = [], []
    for bb in seqs:
        for g in range(NG):
            heads = range(g * G, (g + 1) * G)
            gcol = jnp.concatenate([jnp.broadcast_to(bg[bb][:, H + h:H + h + 1], (C, C)) for h in heads], axis=1)
            bcol = jnp.concatenate([jnp.broadcast_to(bg[bb][:, h:h + 1], (C, C)) for h in heads], axis=1)
            grow = grow_all[bb][:, g * G * C:(g + 1) * G * C]
            decay = jnp.where(incl, jnp.exp(jnp.where(incl, gcol - grow, 0.0)), 0.0)
            kkcat = jnp.concatenate([kk[(bb, h // rep)] for h in heads], axis=1)
            qkcat = jnp.concatenate([qk[(bb, h // rep)] for h in heads], axis=1)
            lowers.append(jnp.where(strict, kkcat * bcol * decay, 0.0))
            intras.append(qkcat * decay)
    tinvs = _unit_lower_inverse(lowers, G)

    def head_block(mats, bb, h):
        return mats[bb * NG + h // G][:, (h % G) * C:(h % G + 1) * C]

    units = [(bb, h) for bb in seqs for h in range(H)]
    g_c = {(bb, h): bg[bb][:, H + h:H + h + 1] for bb, h in units}
    beta = {(bb, h): bg[bb][:, h:h + 1] for bb, h in units}
    g_last = {(bb, h): bg[bb][C - 1:C, H + h:H + h + 1] for bb, h in units}
    kh = {(bb, h): k[(bb, h // rep)] for bb, h in units}
    qh = {(bb, h): q[(bb, h // rep)] for bb, h in units}
    rhs = {(bb, h): jnp.concatenate([v_ref[bb, :, h * HD:(h + 1) * HD] * beta[(bb, h)],
                                     kh[(bb, h)] * beta[(bb, h)] * jnp.exp(g_c[(bb, h)])], axis=1) for bb, h in units}
    sol = {(bb, h): _dot(head_block(tinvs, bb, h), rhs[(bb, h)]) for bb, h in units}
    s = {(bb, h): state_ref[bb * H + h] for bb, h in units}
    v_new = {u: sol[u][:, :HD] - _dot(sol[u][:, HD:], s[u]) for u in units}
    o = {(bb, h): _dot(qh[(bb, h)] * jnp.exp(g_c[(bb, h)]), s[(bb, h)])
         + _dot(head_block(intras, bb, h), v_new[(bb, h)]) for bb, h in units}
    s_new = {u: s[u] * jnp.exp(g_last[u]) + _dot_tn(kh[u] * jnp.exp(g_last[u] - g_c[u]), v_new[u]) for u in units}
    for bb, h in units:
        u = (bb, h)
        state_ref[bb * H + h] = s_new[u]
        on = o[u] * lax.rsqrt(jnp.mean(o[u] * o[u], axis=-1, keepdims=True) + NORM_EPS) * nw
        zh = z_ref[bb, :, h * HD:(h + 1) * HD]
        o_ref[bb, :, h * HD:(h + 1) * HD] = (on * (zh * jax.nn.sigmoid(zh))).astype(o_ref.dtype)


def _gdn_chunk(q, k, v, qkvz, bg, grow, norm_w):
    B, S, _ = q.shape
    nc = S // CHUNK
    zblk = COL_Z // GDN_VAL_DIM
    bb = _seqs_per_step(B)
    return pl.pallas_call(
        _gdn_chunk_kernel,
        grid=(B // bb, nc),
        in_specs=[pl.BlockSpec((bb, CHUNK, GDN_KEY_DIM), lambda b, c: (b, c, 0)),
                  pl.BlockSpec((bb, CHUNK, GDN_KEY_DIM), lambda b, c: (b, c, 0)),
                  pl.BlockSpec((bb, CHUNK, GDN_VAL_DIM), lambda b, c: (b, c, 0)),
                  pl.BlockSpec((bb, CHUNK, GDN_VAL_DIM), lambda b, c: (b, c, zblk)),
                  pl.BlockSpec((bb, CHUNK, LANES), lambda b, c: (b, c, 0)),
                  pl.BlockSpec((bb, None, 1, GDN_V_HEADS * CHUNK), lambda b, c: (b, c, 0, 0)),
                  pl.BlockSpec((1, GDN_HEAD), lambda b, c: (0, 0))],
        out_specs=pl.BlockSpec((bb, CHUNK, GDN_VAL_DIM), lambda b, c: (b, c, 0)),
        out_shape=jax.ShapeDtypeStruct((B, S, GDN_VAL_DIM), bf16),
        scratch_shapes=[pltpu.VMEM((bb * GDN_V_HEADS, GDN_HEAD, GDN_HEAD), f32)],
        compiler_params=_cparams(("parallel", "arbitrary")),
        name="gdn_chunk",
    )(q, k, v, qkvz, bg, grow, norm_w.reshape(1, GDN_HEAD))


def _head_sum(x):
    parts = []
    for s in range(x.shape[1] // LANES):
        xs = x[:, s * LANES:(s + 1) * LANES]
        lo = _iota(xs.shape, 1) < RWKV_HEAD
        s0 = jnp.sum(jnp.where(lo, xs, 0.0), axis=-1, keepdims=True)
        s1 = jnp.sum(jnp.where(lo, 0.0, xs), axis=-1, keepdims=True)
        parts.append(jnp.where(lo, s0, s1))
    return jnp.concatenate(parts, axis=1)


def _rwkv_chunk_kernel(r_ref, k_ref, v_ref, lw_ref, a_ref, gate_ref, kk_ref, ka_ref, rk_ref, lnw_ref, lnb_ref,
                       o_ref, state_ref):
    c_id = pl.program_id(1)

    @pl.when(c_id == 0)
    def _():
        state_ref[...] = jnp.zeros_like(state_ref)

    C = CHUNK
    G = RWKV_GROUP
    W = G * RWKV_HEAD
    n_groups = r_ref.shape[2] // W
    units = [(bb, g) for bb in range(r_ref.shape[0]) for g in range(n_groups)]
    sl = {(bb, g): slice(g * W, (g + 1) * W) for bb, g in units}
    tr = _iota((C, C), 0)
    tc = _iota((C, C), 1)
    tri_incl = jnp.where(tc <= tr, 1.0, 0.0).astype(bf16)
    t = _iota((C, W), 0)
    j = _iota((C, W), 1) % C
    incl = j <= t
    strict = j < t
    same_head = (_iota((W, W), 0) // RWKV_HEAD) == (_iota((W, W), 1) // RWKV_HEAD)
    r = {u: r_ref[u[0], :, sl[u]] for u in units}
    kr = {u: k_ref[u[0], :, sl[u]] for u in units}
    v = {u: v_ref[u[0], :, sl[u]] for u in units}
    lw = {u: lw_ref[u[0], :, sl[u]] for u in units}
    a = {u: a_ref[u[0], :, sl[u]] for u in units}
    cw = {u: _cumsum_rows(tri_incl, lw[u]) for u in units}
    kkv = {u: kr[u] * kk_ref[:, sl[u]] for u in units}
    kk = {u: kkv[u] * lax.rsqrt(_head_sum(kkv[u] * kkv[u]) + 1e-6) for u in units}
    kmod = {u: kr[u] * (1.0 + (a[u] - 1.0) * ka_ref[:, sl[u]]) for u in units}
    bvec = {u: kk[u] * a[u] for u in units}
    cl = {u: cw[u][C - 1:C, :] for u in units}
    e_out = {u: jnp.exp(-cw[u]) for u in units}
    e_end = {u: jnp.exp(cl[u] - cw[u]) for u in units}
    ar = {u: jnp.concatenate([-kk[u] * jnp.exp(cw[u] - lw[u]), r[u] * jnp.exp(cw[u])], axis=0).astype(bf16)
          for u in units}
    fb = {u: _dot_nt(ar[u], _block_diag((bvec[u] * e_out[u]).astype(bf16), G)) for u in units}
    fk = {u: _dot_nt(ar[u], _block_diag((kmod[u] * e_out[u]).astype(bf16), G)) for u in units}
    s = {(bb, g): state_ref[bb * n_groups + g] for bb, g in units}
    ars = {u: _dot_nt(ar[u], s[u]) for u in units}
    vbd = {u: _block_diag(v[u].astype(bf16), G) for u in units}
    tinv = dict(zip(units, _unit_lower_inverse([jnp.where(strict, -fb[u][:C], 0.0) for u in units], G)))
    z = {u: ars[u][:C] + _dot(jnp.where(strict, fk[u][:C], 0.0), vbd[u]) for u in units}
    uu = {u: _cat_matmul(tinv[u], z[u], G) for u in units}
    y = {u: ars[u][C:] + _cat_matmul(jnp.where(incl, fb[u][C:], 0.0), uu[u], G)
         + _dot(jnp.where(incl, fk[u][C:], 0.0), vbd[u]) for u in units}
    s_new = {u: jnp.where(same_head,
                          s[u] * jnp.exp(cl[u]) + _dot_tn(jnp.concatenate([uu[u], v[u]], axis=0),
                                                          jnp.concatenate([bvec[u] * e_end[u], kmod[u] * e_end[u]],
                                                                          axis=0)), 0.0) for u in units}
    for bb, g in units:
        u = (bb, g)
        state_ref[bb * n_groups + g] = s_new[u]
        mu = _head_sum(y[u]) * (1.0 / RWKV_HEAD)
        d = y[u] - mu
        var = _head_sum(d * d) * (1.0 / RWKV_HEAD)
        yn = d * lax.rsqrt(var + RWKV_LN_EPS) * lnw_ref[:, sl[u]] + lnb_ref[:, sl[u]]
        bonus = _head_sum(r[u] * kmod[u] * rk_ref[:, sl[u]]) * v[u]
        o_ref[bb, :, sl[u]] = ((yn + bonus) * gate_ref[bb, :, sl[u]]).astype(o_ref.dtype)


def _rwkv_chunk(r, k, v, lw, a, gate, k_k, k_a, r_k, ln_w, ln_b):
    B, S, R = r.shape
    nc = S // CHUNK
    W = RWKV_GROUP * RWKV_HEAD
    bb = _seqs_per_step(B)
    row = pl.BlockSpec((bb, CHUNK, R), lambda b, c: (b, c, 0))
    par = pl.BlockSpec((1, R), lambda b, c: (0, 0))
    params = [p.reshape(1, R) for p in (k_k, k_a, r_k, ln_w, ln_b)]
    return pl.pallas_call(
        _rwkv_chunk_kernel,
        grid=(B // bb, nc),
        in_specs=[row] * 6 + [par] * 5,
        out_specs=row,
        out_shape=jax.ShapeDtypeStruct((B, S, R), bf16),
        scratch_shapes=[pltpu.VMEM((bb * (R // W), W, W), f32)],
        compiler_params=_cparams(("parallel", "arbitrary")),
        name="rwkv_chunk",
    )(r, k, v, lw, a, gate, *params)


def _merge_kernel(yg_ref, yr_ref, gl_ref, x_ref, gt_ref, nw_ref, wug_ref, wur_ref, wo_ref, o_ref):
    D = x_ref.shape[1]
    sg = jax.nn.sigmoid(gl_ref[...])
    merged = (sg[:, :D] * jnp.dot(yg_ref[...], wug_ref[...], preferred_element_type=f32)
              + sg[:, D:] * jnp.dot(yr_ref[...], wur_ref[...], preferred_element_type=f32))
    out = _dot(merged, wo_ref[...])
    normed = out * lax.rsqrt(jnp.mean(out * out, axis=-1, keepdims=True) + NORM_EPS) * nw_ref[...]
    o_ref[...] = x_ref[...] + gt_ref[...] * normed


def _merge(yg, yr, gates, x, gt, nw, wug, wur, wo, tm=256):
    B, S, D = x.shape
    full = lambda a: pl.BlockSpec(a.shape, lambda b, i: (0,) * a.ndim)
    row = lambda w: pl.BlockSpec((None, tm, w), lambda b, i: (b, i, 0))
    wug, wur, wo = wug.astype(bf16), wur.astype(bf16), wo.astype(bf16)
    nw = nw.reshape(1, D)
    return pl.pallas_call(
        _merge_kernel,
        grid=(B, S // tm),
        in_specs=[row(yg.shape[2]), row(yr.shape[2]), row(2 * D), row(D),
                  pl.BlockSpec((None, 1, D), lambda b, i: (b, 0, 0)), full(nw), full(wug), full(wur), full(wo)],
        out_specs=row(D),
        out_shape=jax.ShapeDtypeStruct((B, S, D), f32),
        compiler_params=_cparams(("parallel", "arbitrary")),
        name="merge",
    )(yg, yr, gates, x, gt, nw, wug, wur, wo)


def _router_kernel(x_ref, sc_ref, sh_ref, nw_ref, whi_ref, wlo_ref, rb_ref, h_ref, sel_ref, prob_ref, cnt_ref,
                   carry_ref, *, n_experts):
    step = pl.program_id(0)

    @pl.when(step == 0)
    def _():
        carry_ref[...] = jnp.zeros_like(carry_ref)

    x = x_ref[...]
    tm = x.shape[0]
    ms = jnp.mean(x * x, axis=-1, keepdims=True)
    h = x * lax.rsqrt(ms + NORM_EPS) * (nw_ref[...] * (1.0 + sc_ref[...])) + sh_ref[...]
    h_ref[...] = _pack_bf16_pairs(h)
    hi = h.astype(bf16)
    lo = (h - hi.astype(f32)).astype(bf16)
    whi = whi_ref[...]
    logits = (jnp.dot(hi, whi, preferred_element_type=f32) + jnp.dot(lo, whi, preferred_element_type=f32)
              + jnp.dot(hi, wlo_ref[...], preferred_element_type=f32)) + rb_ref[...]
    lane = _iota(logits.shape, 1)
    neg = jnp.float32(-jnp.inf)
    cur = jnp.where(lane < n_experts, logits, neg)
    vals, idxs = [], []
    for _ in range(TOP_K):
        m = jnp.max(cur, axis=-1, keepdims=True)
        ix = jnp.min(jnp.where(cur == m, lane, LANES), axis=-1, keepdims=True)
        vals.append(m)
        idxs.append(ix)
        cur = jnp.where(lane == ix, neg, cur)
    es = [jnp.exp(vv - vals[0]) for vv in vals]
    denom = es[0] + es[1] + es[2] + es[3]
    onehots = [lane == ix for ix in idxs]
    msel = jnp.zeros(logits.shape, f32)
    for oh in onehots:
        msel = msel + jnp.where(oh, 1.0, 0.0)
    r = _iota((tm, tm), 0)
    c = _iota((tm, tm), 1)
    tri = jnp.where(c < r, 1.0, 0.0).astype(bf16)
    carry = carry_ref[0:1, :]
    pref = jnp.dot(tri, msel.astype(bf16), preferred_element_type=f32) + carry
    sel = jnp.zeros(logits.shape, i32)
    prob = jnp.zeros(logits.shape, f32)
    for kk in range(TOP_K):
        rank = jnp.sum(jnp.where(onehots[kk], pref, 0.0), axis=-1, keepdims=True).astype(i32)
        sel = jnp.where(lane == kk, idxs[kk], sel)
        sel = jnp.where(lane == TOP_K + kk, rank, sel)
        prob = jnp.where(lane == kk, es[kk] / denom, prob)
    sel_ref[...] = sel
    prob_ref[...] = prob
    new_carry = carry + jnp.sum(msel, axis=0, keepdims=True)
    carry_ref[...] = jnp.broadcast_to(new_carry, carry_ref.shape)
    cnt_ref[...] = jnp.broadcast_to(new_carry, cnt_ref.shape)


def _router(x1, sc, sh, nw, rw, rb, tm=256):
    B, S, D = x1.shape
    E = rw.shape[1]
    rwp = _pad_to(rw, 1, LANES)
    whi = rwp.astype(bf16)
    wlo = (rwp - whi.astype(f32)).astype(bf16)
    rbp = _pad_to(rb.reshape(1, E), 1, LANES)
    spt = S // tm
    row = lambda w: pl.BlockSpec((None, tm, w), lambda i: (i // spt, i % spt, 0))
    per_b = pl.BlockSpec((None, 1, D), lambda i: (i // spt, 0, 0))
    full = lambda a: pl.BlockSpec(a.shape, lambda i: (0,) * a.ndim)
    nw = nw.reshape(1, D)
    return pl.pallas_call(
        functools.partial(_router_kernel, n_experts=E),
        grid=(B * spt,),
        in_specs=[row(D), per_b, per_b, full(nw), full(whi), full(wlo), full(rbp)],
        out_specs=[row(D // 2), row(LANES), row(LANES), pl.BlockSpec((SUBLANES, LANES), lambda i: (0, 0))],
        out_shape=[jax.ShapeDtypeStruct((B, S, D // 2), jnp.uint32), jax.ShapeDtypeStruct((B, S, LANES), i32),
                   jax.ShapeDtypeStruct((B, S, LANES), f32), jax.ShapeDtypeStruct((SUBLANES, LANES), f32)],
        scratch_shapes=[pltpu.VMEM((SUBLANES, LANES), f32)],
        compiler_params=_cparams(("arbitrary",)),
        name="router",
    )(x1, sc, sh, nw, whi, wlo, rbp)


def _dispatch_kernel(pos_ref, h_ref, init_ref, xs_ref, sem):
    del init_ref
    tm = h_ref.shape[0]

    def row_copy(r, kk):
        p = pos_ref[0, r * TOP_K + kk]
        return pltpu.make_async_copy(h_ref.at[pl.ds(r, 1), :], xs_ref.at[pl.ds(p, 1), :], sem)

    def issue(r, carry):
        for kk in range(TOP_K):
            row_copy(r, kk).start(priority=kk % 2)
        return carry

    def drain(r, carry):
        for kk in range(TOP_K):
            row_copy(r, kk).wait()
        return carry

    lax.fori_loop(0, tm, issue, 0)
    lax.fori_loop(0, tm, drain, 0)


def _dispatch(h2, pos, n_rows, tm=256):
    T, D = h2.shape
    pos3 = pos.reshape(T // tm, 1, tm * TOP_K)
    init = jnp.zeros((n_rows, D), h2.dtype)
    return pl.pallas_call(
        _dispatch_kernel,
        grid=(T // tm,),
        in_specs=[pl.BlockSpec((None, 1, tm * TOP_K), lambda i: (i, 0, 0), memory_space=pltpu.SMEM),
                  pl.BlockSpec((tm, D), lambda i: (i, 0)),
                  pl.BlockSpec(memory_space=pl.ANY)],
        out_specs=pl.BlockSpec(memory_space=pl.ANY),
        out_shape=jax.ShapeDtypeStruct((n_rows, D), h2.dtype),
        scratch_shapes=[pltpu.SemaphoreType.DMA],
        input_output_aliases={2: 0},
        compiler_params=_cparams(("arbitrary",)),
        name="dispatch",
    )(pos3, h2, init)


def _expert_kernel(be_ref, nv_ref, nu_ref, x_ref, wg_ref, wu_ref, bg_ref, bu_ref, wd_ref, bd_ref, o_ref, xb_ref,
                   acc_ref):
    del be_ref, nu_ref
    nb = pl.program_id(0)
    f = pl.program_id(1)
    n_valid = nv_ref[nb]

    @pl.when((n_valid > 0) & (f == 0))
    def _():
        xb_ref[...] = _unpack_bf16_pairs(x_ref[...])
        acc_ref[...] = jnp.broadcast_to(bd_ref[...], acc_ref.shape)

    @pl.when((n_valid == 0) & (f == 0))
    def _():
        o_ref[...] = jnp.zeros_like(o_ref)

    n_sub = (n_valid + EXPERT_SUB_ROWS - 1) // EXPERT_SUB_ROWS
    for ns in range(1, EXPERT_ROWS // EXPERT_SUB_ROWS + 1):
        rows = slice(0, ns * EXPERT_SUB_ROWS)

        @pl.when(n_sub == ns)
        def _():
            x = xb_ref[rows, :]
            g = jnp.dot(x, wg_ref[...].astype(bf16), preferred_element_type=f32) + bg_ref[...]
            u = jnp.dot(x, wu_ref[...].astype(bf16), preferred_element_type=f32) + bu_ref[...]
            gl = jnp.minimum(g, SWIGLU_LIMIT)
            up = jnp.clip(u, -SWIGLU_LIMIT, SWIGLU_LIMIT)
            act = gl * jax.nn.sigmoid(SWIGLU_ALPHA * gl) * (up + 1.0)
            acc_ref[rows, :] += jnp.dot(act.astype(bf16), wd_ref[...].astype(bf16), preferred_element_type=f32)

    @pl.when((n_valid > 0) & (f == pl.num_programs(1) - 1))
    def _():
        o_ref[...] = _pack_bf16_pairs(acc_ref[...])


def _experts(xs, block_e, n_valid, n_used, w_gu, b_gu, w_down, b_down):
    P, Dh = xs.shape
    D = 2 * Dh
    E, _, F2 = w_gu.shape
    F = F2 // 2
    tf = EXPERT_FF_TILE
    nf = F // tf
    NB = P // EXPERT_ROWS
    b_gu3 = b_gu.reshape(E, 1, F2)
    b_dn3 = b_down.reshape(E, 1, D)

    def blk(nb, nu):
        return jnp.minimum(nb, nu[0] - 1)

    def fidx(nb, f, nu):
        return jnp.where(nb < nu[0], f, nf - 1)

    grid_spec = pltpu.PrefetchScalarGridSpec(
        num_scalar_prefetch=3,
        grid=(NB, nf),
        in_specs=[
            pl.BlockSpec((EXPERT_ROWS, Dh), lambda nb, f, be, nv, nu: (blk(nb, nu), 0)),
            pl.BlockSpec((None, D, tf), lambda nb, f, be, nv, nu: (be[blk(nb, nu)], 0, fidx(nb, f, nu))),
            pl.BlockSpec((None, D, tf), lambda nb, f, be, nv, nu: (be[blk(nb, nu)], 0, nf + fidx(nb, f, nu))),
            pl.BlockSpec((None, 1, tf), lambda nb, f, be, nv, nu: (be[blk(nb, nu)], 0, fidx(nb, f, nu))),
            pl.BlockSpec((None, 1, tf), lambda nb, f, be, nv, nu: (be[blk(nb, nu)], 0, nf + fidx(nb, f, nu))),
            pl.BlockSpec((None, tf, D), lambda nb, f, be, nv, nu: (be[blk(nb, nu)], fidx(nb, f, nu), 0)),
            pl.BlockSpec((None, 1, D), lambda nb, f, be, nv, nu: (be[blk(nb, nu)], 0, 0)),
        ],
        out_specs=pl.BlockSpec((EXPERT_ROWS, Dh), lambda nb, f, be, nv, nu: (nb, 0)),
        scratch_shapes=[pltpu.VMEM((EXPERT_ROWS, D), bf16), pltpu.VMEM((EXPERT_ROWS, D), f32)],
    )
    return pl.pallas_call(
        _expert_kernel,
        grid_spec=grid_spec,
        out_shape=jax.ShapeDtypeStruct((P, Dh), jnp.uint32),
        compiler_params=_cparams(("arbitrary", "arbitrary")),
        name="experts",
    )(block_e, n_valid, n_used, xs, w_gu, w_gu, b_gu3, b_gu3, w_down, b_dn3)


def _combine_kernel(pos_ref, pos_next_ref, yb_ref, prob_ref, x_ref, gt_ref, nw_ref, o_ref, buf_ref, sems):
    tm = x_ref.shape[0]
    step = pl.program_id(0)
    slot = step % 2

    def row_copy(idx_ref, s, r, kk):
        p = idx_ref[0, r * TOP_K + kk]
        return pltpu.make_async_copy(yb_ref.at[pl.ds(p, 1), :], buf_ref.at[s, kk, pl.ds(r, 1), :], sems.at[s])

    def issue_tile(idx_ref, s):
        def issue(r, carry):
            for kk in range(TOP_K):
                row_copy(idx_ref, s, r, kk).start(priority=kk % 2)
            return carry

        lax.fori_loop(0, tm, issue, 0)

    @pl.when(step == 0)
    def _():
        issue_tile(pos_ref, 0)

    @pl.when(step + 1 < pl.num_programs(0))
    def _():
        issue_tile(pos_next_ref, 1 - slot)

    def drain(r, carry):
        for kk in range(TOP_K):
            row_copy(pos_ref, slot, r, kk).wait()
        return carry

    lax.fori_loop(0, tm, drain, 0)
    prob = prob_ref[...]
    y_lo = None
    for kk in range(TOP_K):
        u = buf_ref[slot, kk]
        lo = lax.bitcast_convert_type(u << 16, f32) * prob[:, kk:kk + 1]
        hi = lax.bitcast_convert_type(u & jnp.uint32(0xFFFF0000), f32) * prob[:, kk:kk + 1]
        y_lo, y_hi = (lo, hi) if y_lo is None else (y_lo + lo, y_hi + hi)
    y = jnp.concatenate([y_lo, y_hi], axis=1)
    normed = y * lax.rsqrt(jnp.mean(y * y, axis=-1, keepdims=True) + NORM_EPS) * nw_ref[...]
    o_ref[...] = x_ref[...] + gt_ref[...] * normed


def _combine(yb, pos, prob, x1, gt, nw, tm=256):
    B, S, D = x1.shape
    T = B * S
    spt = S // tm
    pos3 = pos.reshape(T // tm, 1, tm * TOP_K)
    row = lambda w: pl.BlockSpec((None, tm, w), lambda i: (i // spt, i % spt, 0))
    nw = nw.reshape(1, D)
    n_tiles = T // tm
    return pl.pallas_call(
        _combine_kernel,
        grid=(n_tiles,),
        in_specs=[pl.BlockSpec((None, 1, tm * TOP_K), lambda i: (i, 0, 0), memory_space=pltpu.SMEM),
                  pl.BlockSpec((None, 1, tm * TOP_K), lambda i: (jnp.minimum(i + 1, n_tiles - 1), 0, 0),
                               memory_space=pltpu.SMEM),
                  pl.BlockSpec(memory_space=pl.ANY),
                  row(LANES), row(D),
                  pl.BlockSpec((None, 1, D), lambda i: (i // spt, 0, 0)),
                  pl.BlockSpec(nw.shape, lambda i: (0, 0))],
        out_specs=row(D),
        out_shape=jax.ShapeDtypeStruct((B, S, D), f32),
        scratch_shapes=[pltpu.VMEM((2, TOP_K, tm, D // 2), jnp.uint32), pltpu.SemaphoreType.DMA((2,))],
        compiler_params=_cparams(("arbitrary",)),
        name="combine",
    )(pos3, pos3, yb, prob, x1, gt, nw)


def _mixer(x, sc1, sh1, gt1, norm_pre, norm_post, w_in, gdn_conv, gdn_A_log, gdn_dt_bias, gdn_norm, rwkv_mix,
           rwkv_w0, rwkv_w1, rwkv_w2, rwkv_a0, rwkv_a1, rwkv_a2, rwkv_g1, rwkv_g2, rwkv_k_k, rwkv_k_a, rwkv_r_k,
           rwkv_ln_w, rwkv_ln_b, w_up_gdn, w_up_rwkv, w_out):
    B, S, D = x.shape
    T = B * S
    R = rwkv_w2.shape[1]
    col_a = COL_B + GDN_V_HEADS
    col_gate = col_a + GDN_V_HEADS
    col_r = col_gate + 2 * D
    h, xr, xk, xv, lw, a_sig, gate = _premix(x, sc1, sh1, norm_pre, rwkv_mix, rwkv_w1, rwkv_w2, rwkv_w0, rwkv_a1,
                                             rwkv_a2, rwkv_a0, rwkv_g1, rwkv_g2)
    wb = w_in.astype(bf16)
    h2d = h.reshape(T, D)
    qkvz = _mm(h2d, wb[:, :COL_B]).reshape(B, S, COL_B)
    ba = _mm(h2d, _pad_to(wb[:, COL_B:col_gate], 1, LANES)).reshape(B, S, LANES)
    gates = _mm(h2d, wb[:, col_gate:col_r]).reshape(B, S, 2 * D)
    r = _mm(xr.reshape(T, D), wb[:, col_r:col_r + R]).reshape(B, S, R)
    k_r = _mm(xk.reshape(T, D), wb[:, col_r + R:col_r + 2 * R]).reshape(B, S, R)
    v_r = _mm(xv.reshape(T, D), wb[:, col_r + 2 * R:col_r + 3 * R]).reshape(B, S, R)

    q, k, v, bg = _gdn_prep(qkvz, ba, gdn_conv, gdn_A_log, gdn_dt_bias)
    nc = S // CHUNK
    grow = bg[:, :, GDN_V_HEADS:2 * GDN_V_HEADS].reshape(B, nc, CHUNK, GDN_V_HEADS)
    grow = grow.transpose(0, 1, 3, 2).reshape(B, nc, 1, GDN_V_HEADS * CHUNK)
    y_gdn = _gdn_chunk(q, k, v, qkvz, bg, grow, gdn_norm)
    y_rwkv = _rwkv_chunk(r, k_r, v_r, lw, a_sig, gate, rwkv_k_k, rwkv_k_a, rwkv_r_k.reshape(-1), rwkv_ln_w,
                         rwkv_ln_b)
    return _merge(y_gdn, y_rwkv, gates, x, gt1, norm_post, w_up_gdn, w_up_rwkv, w_out)


def _moe(x1, sc2, sh2, gt2, norm_pre, norm_post, router_w, router_b, w_gu, b_gu, w_down, b_down):
    B, S, D = x1.shape
    T = B * S
    E = router_w.shape[1]
    h2, sel, prob, cnt = _router(x1, sc2, sh2, norm_pre, router_w, router_b)
    sel = sel.reshape(T, LANES)
    e_idx = sel[:, :TOP_K]
    rank = sel[:, TOP_K:2 * TOP_K]
    counts = cnt[0, :E].astype(i32)
    padded = (counts + EXPERT_ROWS - 1) // EXPERT_ROWS * EXPERT_ROWS
    pad_end = jnp.cumsum(padded)
    pad_start = pad_end - padded
    pos = (pad_start[e_idx] + rank).astype(i32)
    P = T * TOP_K + E * EXPERT_ROWS
    NB = P // EXPERT_ROWS
    block_start = jnp.arange(NB, dtype=i32) * EXPERT_ROWS
    block_e = jnp.minimum(jnp.sum(pad_end[None, :] <= block_start[:, None], axis=1), E - 1).astype(i32)
    n_valid = jnp.clip(pad_start[block_e] + counts[block_e] - block_start, 0, EXPERT_ROWS).astype(i32)
    n_used = jnp.maximum(pad_end[-1:] // EXPERT_ROWS, 1).astype(i32)
    xs = _dispatch(h2.reshape(T, D // 2), pos, P)
    yb = _experts(xs, block_e, n_valid, n_used, w_gu, b_gu, w_down, b_down)
    return _combine(yb, pos, prob, x1, gt2, norm_post)


def kernel(x, c, ada_w, ada_b, norm_mix_pre, norm_mix_post, norm_ffn_pre, norm_ffn_post, w_in, gdn_conv, gdn_A_log,
           gdn_dt_bias, gdn_norm, rwkv_mix, rwkv_w0, rwkv_w1, rwkv_w2, rwkv_a0, rwkv_a1, rwkv_a2, rwkv_g1, rwkv_g2,
           rwkv_k_k, rwkv_k_a, rwkv_r_k, rwkv_ln_w, rwkv_ln_b, w_up_gdn, w_up_rwkv, w_out, router_w, router_b,
           exp_w_gu, exp_b_gu, exp_w_down, exp_b_down):
    B, S, D = x.shape
    for l in range(ada_w.shape[0]):
        mod = _ada(c, ada_w[l], ada_b[l])
        sh1, sc1, gt1, sh2, sc2, gt2 = [m.reshape(B, 1, D) for m in jnp.split(mod, 6, axis=-1)]
        x = _mixer(x, sc1, sh1, gt1, norm_mix_pre[l], norm_mix_post[l], w_in[l], gdn_conv[l], gdn_A_log[l],
                   gdn_dt_bias[l], gdn_norm[l], rwkv_mix[l], rwkv_w0[l], rwkv_w1[l], rwkv_w2[l], rwkv_a0[l],
                   rwkv_a1[l], rwkv_a2[l], rwkv_g1[l], rwkv_g2[l], rwkv_k_k[l], rwkv_k_a[l], rwkv_r_k[l],
                   rwkv_ln_w[l], rwkv_ln_b[l], w_up_gdn[l], w_up_rwkv[l], w_out[l])
        x = _moe(x, sc2, sh2, gt2, norm_ffn_pre[l], norm_ffn_post[l], router_w[l], router_b[l], exp_w_gu[l],
                 exp_b_gu[l], exp_w_down[l], exp_b_down[l])
    return x
```

```python
import functools

import jax
import jax.numpy as jnp
from jax import lax
from jax.experimental import pallas as pl
from jax.experimental.pallas import tpu as pltpu

f32 = jnp.float32
bf16 = jnp.bfloat16
i32 = jnp.int32

NORM_EPS = 1e-6
CHUNK = 64
GDN_QK_HEADS = 4
GDN_V_HEADS = 8
GDN_HEAD = 128
GDN_KEY_DIM = GDN_QK_HEADS * GDN_HEAD
GDN_VAL_DIM = GDN_V_HEADS * GDN_HEAD
GDN_CONV = 4
RWKV_HEAD = 64
RWKV_LN_EPS = 64e-5
RWKV_GROUP = 2
TOP_K = 4
SWIGLU_ALPHA = 1.702
SWIGLU_LIMIT = 7.0
COL_Z = 2 * GDN_KEY_DIM + GDN_VAL_DIM
COL_B = COL_Z + GDN_VAL_DIM

LANES = 128
SUBLANES = 8
VMEM_LIMIT_BYTES = 60 * 1024 * 1024

EXPERT_ROWS = 1024
EXPERT_SUB_ROWS = 256
EXPERT_FF_TILE = 256
INV_BLOCK = 16


def _cparams(sem):
    return pltpu.CompilerParams(dimension_semantics=sem, vmem_limit_bytes=VMEM_LIMIT_BYTES)


def _dot(a, b):
    return jnp.dot(a.astype(bf16), b.astype(bf16), preferred_element_type=f32)


def _dot_nt(a, b):
    return lax.dot_general(a.astype(bf16), b.astype(bf16), (((1,), (1,)), ((), ())), preferred_element_type=f32)


def _dot_tn(a, b):
    return lax.dot_general(a.astype(bf16), b.astype(bf16), (((0,), (0,)), ((), ())), preferred_element_type=f32)


def _dot_f32(a, b):
    return jnp.dot(a, b, preferred_element_type=f32, precision=lax.Precision.HIGHEST)


def _softplus(z):
    return jnp.maximum(z, 0.0) + jnp.log1p(jnp.exp(-jnp.abs(z)))


def _iota(shape, axis):
    return lax.broadcasted_iota(i32, shape, axis)


def _pack_bf16_pairs(h):
    half = h.shape[1] // 2
    lo = lax.bitcast_convert_type(h[:, :half].astype(bf16).astype(f32), jnp.uint32) >> 16
    hi = lax.bitcast_convert_type(h[:, half:].astype(bf16).astype(f32), jnp.uint32) & jnp.uint32(0xFFFF0000)
    return lo | hi


def _unpack_bf16_pairs(u):
    lo = lax.bitcast_convert_type(u << 16, f32).astype(bf16)
    hi = lax.bitcast_convert_type(u & jnp.uint32(0xFFFF0000), f32).astype(bf16)
    return jnp.concatenate([lo, hi], axis=1)


def _ada_kernel(c_ref, w_ref, b_ref, o_ref):
    c = c_ref[...]
    s = c * jax.nn.sigmoid(c)
    o_ref[...] = _dot(s, w_ref[...]) + b_ref[...]


def _ada(c, w, b):
    B, D = c.shape
    N = w.shape[1]
    Bp = -(-B // SUBLANES) * SUBLANES
    cp = jnp.pad(c, ((0, Bp - B), (0, 0)))
    tn = 1024
    out = pl.pallas_call(
        _ada_kernel,
        grid=(N // tn,),
        in_specs=[pl.BlockSpec((Bp, D), lambda j: (0, 0)),
                  pl.BlockSpec((D, tn), lambda j: (0, j)),
                  pl.BlockSpec((1, tn), lambda j: (0, j))],
        out_specs=pl.BlockSpec((Bp, tn), lambda j: (0, j)),
        out_shape=jax.ShapeDtypeStruct((Bp, N), f32),
        compiler_params=_cparams(("arbitrary",)),
        name="ada",
    )(cp, w, b.reshape(1, N))
    return out[:B]


def _premix_kernel(x_ref, xp_ref, sc_ref, sh_ref, nw_ref, mix_ref, w1_ref, w2_ref, w0_ref, a1_ref, a2_ref, a0_ref,
                   g1_ref, g2_ref, h_ref, xr_ref, xk_ref, xv_ref, lw_ref, a_ref, gate_ref):
    i = pl.program_id(1)
    scale = nw_ref[...] * (1.0 + sc_ref[...])
    shift = sh_ref[...]

    def modulated_norm(x):
        ms = jnp.mean(x * x, axis=-1, keepdims=True)
        return x * lax.rsqrt(ms + NORM_EPS) * scale + shift

    h = modulated_norm(x_ref[...])
    hp = modulated_norm(xp_ref[...])[SUBLANES - 1:SUBLANES, :]
    hp = jnp.where(i == 0, 0.0, hp)
    rows = _iota(h.shape, 0)
    hprev = jnp.where(rows == 0, hp, pltpu.roll(h, 1, 0))
    dx = hprev - h
    mix = mix_ref[...]
    h_ref[...] = h.astype(bf16)
    xr_ref[...] = (h + dx * mix[0:1]).astype(bf16)
    xk_ref[...] = (h + dx * mix[2:3]).astype(bf16)
    xv_ref[...] = (h + dx * mix[3:4]).astype(bf16)
    xw = h + dx * mix[1:2]
    xa = h + dx * mix[4:5]
    xg = h + dx * mix[5:6]
    wl = w0_ref[...] + _dot(jnp.tanh(_dot(xw, w1_ref[...])), w2_ref[...])
    w_log = -_softplus(-wl) - 0.5
    lw_ref[...] = -jnp.exp(w_log)
    a_ref[...] = jax.nn.sigmoid(a0_ref[...] + _dot(_dot(xa, a1_ref[...]), a2_ref[...]))
    gate_ref[...] = _dot(jax.nn.sigmoid(_dot(xg, g1_ref[...])), g2_ref[...])


def _pad_to(a, axis, mult):
    n = a.shape[axis]
    npad = -(-n // mult) * mult - n
    if npad == 0:
        return a
    pads = [(0, 0)] * a.ndim
    pads[axis] = (0, npad)
    return jnp.pad(a, pads)


def _premix(x, sc, sh, nw, mix, w1, w2, w0, a1, a2, a0, g1, g2, tm=256):
    B, S, D = x.shape
    R = w2.shape[1]
    w1p = _pad_to(w1, 1, LANES).astype(bf16)
    w2p = _pad_to(w2, 0, LANES).astype(bf16)
    a1p = _pad_to(a1, 1, LANES).astype(bf16)
    a2p = _pad_to(a2, 0, LANES).astype(bf16)
    g1p = g1.astype(bf16)
    g2p = g2.astype(bf16)
    mixp = _pad_to(mix, 0, SUBLANES)
    nb = tm // SUBLANES

    def full(a):
        return pl.BlockSpec(a.shape, lambda b, i: (0,) * a.ndim)

    row = lambda w, dt: (pl.BlockSpec((None, tm, w), lambda b, i: (b, i, 0)), jax.ShapeDtypeStruct((B, S, w), dt))
    outs = [row(D, bf16), row(D, bf16), row(D, bf16), row(D, bf16), row(R, f32), row(R, f32), row(R, f32)]
    per_b = pl.BlockSpec((None, 1, D), lambda b, i: (b, 0, 0))
    args = (x, x, sc, sh, nw.reshape(1, D), mixp, w1p, w2p, w0.reshape(1, R), a1p, a2p, a0.reshape(1, R), g1p, g2p)
    in_specs = [pl.BlockSpec((None, tm, D), lambda b, i: (b, i, 0)),
                pl.BlockSpec((None, SUBLANES, D), lambda b, i: (b, jnp.maximum(i * nb - 1, 0), 0)),
                per_b, per_b] + [full(a) for a in args[4:]]
    return pl.pallas_call(
        _premix_kernel,
        grid=(B, S // tm),
        in_specs=in_specs,
        out_specs=[o[0] for o in outs],
        out_shape=[o[1] for o in outs],
        compiler_params=_cparams(("parallel", "arbitrary")),
        name="premix",
    )(*args)


def _mm_kernel(a_ref, b_ref, o_ref):
    o_ref[...] = jnp.dot(a_ref[...], b_ref[...], preferred_element_type=f32)


def _mm(a, b, tm=1024, tn=1024):
    M, K = a.shape
    N = b.shape[1]
    tm = min(tm, M)
    tn = min(tn, N)
    assert M % tm == 0 and N % tn == 0
    return pl.pallas_call(
        _mm_kernel,
        grid=(M // tm, N // tn),
        in_specs=[pl.BlockSpec((tm, K), lambda i, j: (i, 0)),
                  pl.BlockSpec((K, tn), lambda i, j: (0, j))],
        out_specs=pl.BlockSpec((tm, tn), lambda i, j: (i, j)),
        out_shape=jax.ShapeDtypeStruct((M, N), f32),
        compiler_params=_cparams(("parallel", "arbitrary")),
        name="mm",
    )(a, b)


def _gdn_prep_kernel(x_ref, xp_ref, ba_ref, cw_ref, alog_ref, dtb_ref, q_ref, k_ref, v_ref, bg_ref):
    i = pl.program_id(1)
    x = x_ref[...]
    tm = x.shape[0]
    xp = jnp.where(i == 0, 0.0, xp_ref[...])
    cw = cw_ref[...]
    rows8 = _iota(xp.shape, 0)
    acc = x * cw[GDN_CONV - 1:GDN_CONV]
    for j in range(1, GDN_CONV):
        rolled = pltpu.roll(x, j, 0)
        head = jnp.where(rows8 < j, pltpu.roll(xp, j, 0), rolled[0:SUBLANES])
        shifted = jnp.concatenate([head, rolled[SUBLANES:]], axis=0)
        acc = acc + shifted * cw[GDN_CONV - 1 - j:GDN_CONV - j]
    y = acc * jax.nn.sigmoid(acc)

    def l2n(t):
        return t * lax.rsqrt(jnp.sum(t * t, axis=-1, keepdims=True) + 1e-6)

    for hq in range(GDN_QK_HEADS):
        sl = slice(hq * GDN_HEAD, (hq + 1) * GDN_HEAD)
        q_ref[:, sl] = l2n(y[:, sl]) * (GDN_HEAD ** -0.5)
        ks = slice(GDN_KEY_DIM + hq * GDN_HEAD, GDN_KEY_DIM + (hq + 1) * GDN_HEAD)
        k_ref[:, sl] = l2n(y[:, ks])
    v_ref[...] = y[:, 2 * GDN_KEY_DIM:]

    ba = ba_ref[...]
    lane = _iota(ba.shape, 1)
    beta = jax.nn.sigmoid(ba)
    gstep = -jnp.exp(alog_ref[...]) * _softplus(ba + dtb_ref[...])
    gstep = jnp.where((lane >= GDN_V_HEADS) & (lane < 2 * GDN_V_HEADS), gstep, 0.0)
    r = _iota((tm, tm), 0)
    c = _iota((tm, tm), 1)
    tri = jnp.where((c <= r) & (r // CHUNK == c // CHUNK), 1.0, 0.0).astype(f32)
    gcum = _dot_f32(tri, gstep)
    bg_ref[...] = jnp.where(lane < GDN_V_HEADS, beta, gcum)


def _gdn_prep(qkv, ba, conv_w, a_log, dt_bias, tm=256):
    B, S, _ = qkv.shape
    C2 = 2 * GDN_KEY_DIM + GDN_VAL_DIM
    cwp = _pad_to(conv_w, 0, SUBLANES)
    alog = jnp.zeros((1, LANES), f32).at[0, GDN_V_HEADS:2 * GDN_V_HEADS].set(a_log)
    dtb = jnp.zeros((1, LANES), f32).at[0, GDN_V_HEADS:2 * GDN_V_HEADS].set(dt_bias)
    nb = tm // SUBLANES
    row = lambda w: (pl.BlockSpec((None, tm, w), lambda b, i: (b, i, 0)), jax.ShapeDtypeStruct((B, S, w), f32))
    outs = [row(GDN_KEY_DIM), row(GDN_KEY_DIM), row(GDN_VAL_DIM), row(LANES)]
    return pl.pallas_call(
        _gdn_prep_kernel,
        grid=(B, S // tm),
        in_specs=[pl.BlockSpec((None, tm, C2), lambda b, i: (b, i, 0)),
                  pl.BlockSpec((None, SUBLANES, C2), lambda b, i: (b, jnp.maximum(i * nb - 1, 0), 0)),
                  pl.BlockSpec((None, tm, LANES), lambda b, i: (b, i, 0)),
                  pl.BlockSpec(cwp.shape, lambda b, i: (0, 0)),
                  pl.BlockSpec((1, LANES), lambda b, i: (0, 0)),
                  pl.BlockSpec((1, LANES), lambda b, i: (0, 0))],
        out_specs=[o[0] for o in outs],
        out_shape=[o[1] for o in outs],
        compiler_params=_cparams(("parallel", "arbitrary")),
        name="gdn_prep",
    )(qkv, qkv, ba, cwp, alog, dtb)


def _seqs_per_step(batch):
    return next(n for n in (4, 2, 1) if batch % n == 0)


def _cumsum_rows(tri, x):
    hi = x.astype(bf16)
    r1 = x - hi.astype(f32)
    mid = r1.astype(bf16)
    lo = (r1 - mid.astype(f32)).astype(bf16)
    return (jnp.dot(tri, hi, preferred_element_type=f32) + jnp.dot(tri, mid, preferred_element_type=f32)
            + jnp.dot(tri, lo, preferred_element_type=f32))


def _block_diag(y, nblk):
    C, W = y.shape
    w = W // nblk
    t = jnp.concatenate([y] * nblk, axis=0)
    r = _iota(t.shape, 0) // C
    c = _iota(t.shape, 1) // w
    return jnp.where(r == c, t, jnp.zeros_like(t))


def _cat_matmul(xcat, ycat, nblk):
    return _dot(xcat, _block_diag(ycat.astype(bf16), nblk))


def _unit_lower_inverse(lcats, nblk):
    C = lcats[0].shape[0]
    assert C // INV_BLOCK == 4
    t = _iota(lcats[0].shape, 0)
    j = _iota(lcats[0].shape, 1) % C
    eye = jnp.where(t == j, 1.0, 0.0).astype(f32)
    in_diag = (t // INV_BLOCK) == (j // INV_BLOCK)
    dparts = [jnp.where(in_diag, l, 0.0) for l in lcats]
    oparts = [l - d for l, d in zip(lcats, dparts)]
    ps = [-d for d in dparts]
    xs = [eye + p for p in ps]
    levels = INV_BLOCK.bit_length() - 2
    for _ in range(levels):
        ps = [_cat_matmul(p, p, nblk) for p in ps]
        xs = [x + _cat_matmul(x, p, nblk) for x, p in zip(xs, ps)]
    ns = [_cat_matmul(x, o, nblk) for x, o in zip(xs, oparts)]
    n2s = [_cat_matmul(n, n, nblk) for n in ns]
    n3s = [_cat_matmul(n, n2, nblk) for n, n2 in zip(ns, n2s)]
    return [_cat_matmul(eye - n + n2 - n3, x, nblk) for n, n2, n3, x in zip(ns, n2s, n3s, xs)]


def _gdn_chunk_kernel(q_ref, k_ref, v_ref, z_ref, bg_ref, grow_ref, nw_ref, o_ref, state_ref):
    c_id = pl.program_id(1)

    @pl.when(c_id == 0)
    def _():
        state_ref[...] = jnp.zeros_like(state_ref)

    C = CHUNK
    H = GDN_V_HEADS
    HQ = GDN_QK_HEADS
    HD = GDN_HEAD
    rep = H // HQ
    G = 2
    NG = H // G
    seqs = range(q_ref.shape[0])
    nw = nw_ref[...]
    t = _iota((C, G * C), 0)
    j = _iota((C, G * C), 1) % C
    incl = j <= t
    strict = j < t
    bg = [bg_ref[bb] for bb in seqs]
    grow_all = [grow_ref[bb] for bb in seqs]
    qk_units = [(bb, h) for bb in seqs for h in range(HQ)]
    q = {(bb, h): q_ref[bb, :, h * HD:(h + 1) * HD] for bb, h in qk_units}
    k = {(bb, h): k_ref[bb, :, h * HD:(h + 1) * HD] for bb, h in qk_units}
    kk = {u: _dot_nt(k[u], k[u]) for u in qk_units}
    qk = {u: _dot_nt(q[u], k[u]) for u in qk_units}
    lowers, intras = [], []
    for bb in seqs:
        for g in range(NG):
            heads = range(g * G, (g + 1) * G)
            gcol = jnp.concatenate([jnp.broadcast_to(bg[bb][:, H + h:H + h + 1], (C, C)) for h in heads], axis=1)
            bcol = jnp.concatenate([jnp.broadcast_to(bg[bb][:, h:h + 1], (C, C)) for h in heads], axis=1)
            grow = grow_all[bb][:, g * G * C:(g + 1) * G * C]
            decay = jnp.where(incl, jnp.exp(jnp.where(incl, gcol - grow, 0.0)), 0.0)
            kkcat = jnp.concatenate([kk[(bb, h // rep)] for h in heads], axis=1)
            qkcat = jnp.concatenate([qk[(bb, h // rep)] for h in heads], axis=1)
            lowers.append(jnp.where(strict, kkcat * bcol * decay, 0.0))
            intras.append(qkcat * decay)
    tinvs = _unit_lower_inverse(lowers, G)

    def head_block(mats, bb, h):
        return mats[bb * NG + h // G][:, (h % G) * C:(h % G + 1) * C]

    units = [(bb, h) for bb in seqs for h in range(H)]
    g_c = {(bb, h): bg[bb][:, H + h:H + h + 1] for bb, h in units}
    beta = {(bb, h): bg[bb][:, h:h + 1] for bb, h in units}
    g_last = {(bb, h): bg[bb][C - 1:C, H + h:H + h + 1] for bb, h in units}
    kh = {(bb, h): k[(bb, h // rep)] for bb, h in units}
    qh = {(bb, h): q[(bb, h // rep)] for bb, h in units}
    rhs = {(bb, h): jnp.concatenate([v_ref[bb, :, h * HD:(h + 1) * HD] * beta[(bb, h)],
                                     kh[(bb, h)] * beta[(bb, h)] * jnp.exp(g_c[(bb, h)])], axis=1) for bb, h in units}
    sol = {(bb, h): _dot(head_block(tinvs, bb, h), rhs[(bb, h)]) for bb, h in units}
    s = {(bb, h): state_ref[bb * H + h] for bb, h in units}
    v_new = {u: sol[u][:, :HD] - _dot(sol[u][:, HD:], s[u]) for u in units}
    o = {(bb, h): _dot(qh[(bb, h)] * jnp.exp(g_c[(bb, h)]), s[(bb, h)])
         + _dot(head_block(intras, bb, h), v_new[(bb, h)]) for bb, h in units}
    s_new = {u: s[u] * jnp.exp(g_last[u]) + _dot_tn(kh[u] * jnp.exp(g_last[u] - g_c[u]), v_new[u]) for u in units}
    for bb, h in units:
        u = (bb, h)
        state_ref[bb * H + h] = s_new[u]
        on = o[u] * lax.rsqrt(jnp.mean(o[u] * o[u], axis=-1, keepdims=True) + NORM_EPS) * nw
        zh = z_ref[bb, :, h * HD:(h + 1) * HD]
        o_ref[bb, :, h * HD:(h + 1) * HD] = (on * (zh * jax.nn.sigmoid(zh))).astype(o_ref.dtype)


def _gdn_chunk(q, k, v, qkvz, bg, grow, norm_w):
    B, S, _ = q.shape
    nc = S // CHUNK
    zblk = COL_Z // GDN_VAL_DIM
    bb = _seqs_per_step(B)
    return pl.pallas_call(
        _gdn_chunk_kernel,
        grid=(B // bb, nc),
        in_specs=[pl.BlockSpec((bb, CHUNK, GDN_KEY_DIM), lambda b, c: (b, c, 0)),
                  pl.BlockSpec((bb, CHUNK, GDN_KEY_DIM), lambda b, c: (b, c, 0)),
                  pl.BlockSpec((bb, CHUNK, GDN_VAL_DIM), lambda b, c: (b, c, 0)),
                  pl.BlockSpec((bb, CHUNK, GDN_VAL_DIM), lambda b, c: (b, c, zblk)),
                  pl.BlockSpec((bb, CHUNK, LANES), lambda b, c: (b, c, 0)),
                  pl.BlockSpec((bb, None, 1, GDN_V_HEADS * CHUNK), lambda b, c: (b, c, 0, 0)),
                  pl.BlockSpec((1, GDN_HEAD), lambda b, c: (0, 0))],
        out_specs=pl.BlockSpec((bb, CHUNK, GDN_VAL_DIM), lambda b, c: (b, c, 0)),
        out_shape=jax.ShapeDtypeStruct((B, S, GDN_VAL_DIM), bf16),
        scratch_shapes=[pltpu.VMEM((bb * GDN_V_HEADS, GDN_HEAD, GDN_HEAD), f32)],
        compiler_params=_cparams(("parallel", "arbitrary")),
        name="gdn_chunk",
    )(q, k, v, qkvz, bg, grow, norm_w.reshape(1, GDN_HEAD))


def _head_sum(x):
    parts = []
    for s in range(x.shape[1] // LANES):
        xs = x[:, s * LANES:(s + 1) * LANES]
        lo = _iota(xs.shape, 1) < RWKV_HEAD
        s0 = jnp.sum(jnp.where(lo, xs, 0.0), axis=-1, keepdims=True)
        s1 = jnp.sum(jnp.where(lo, 0.0, xs), axis=-1, keepdims=True)
        parts.append(jnp.where(lo, s0, s1))
    return jnp.concatenate(parts, axis=1)


def _rwkv_chunk_kernel(r_ref, k_ref, v_ref, lw_ref, a_ref, gate_ref, kk_ref, ka_ref, rk_ref, lnw_ref, lnb_ref,
                       o_ref, state_ref):
    c_id = pl.program_id(1)

    @pl.when(c_id == 0)
    def _():
        state_ref[...] = jnp.zeros_like(state_ref)

    C = CHUNK
    G = RWKV_GROUP
    W = G * RWKV_HEAD
    n_groups = r_ref.shape[2] // W
    units = [(bb, g) for bb in range(r_ref.shape[0]) for g in range(n_groups)]
    sl = {(bb, g): slice(g * W, (g + 1) * W) for bb, g in units}
    tr = _iota((C, C), 0)
    tc = _iota((C, C), 1)
    tri_incl = jnp.where(tc <= tr, 1.0, 0.0).astype(bf16)
    t = _iota((C, W), 0)
    j = _iota((C, W), 1) % C
    incl = j <= t
    strict = j < t
    same_head = (_iota((W, W), 0) // RWKV_HEAD) == (_iota((W, W), 1) // RWKV_HEAD)
    r = {u: r_ref[u[0], :, sl[u]] for u in units}
    kr = {u: k_ref[u[0], :, sl[u]] for u in units}
    v = {u: v_ref[u[0], :, sl[u]] for u in units}
    lw = {u: lw_ref[u[0], :, sl[u]] for u in units}
    a = {u: a_ref[u[0], :, sl[u]] for u in units}
    cw = {u: _cumsum_rows(tri_incl, lw[u]) for u in units}
    kkv = {u: kr[u] * kk_ref[:, sl[u]] for u in units}
    kk = {u: kkv[u] * lax.rsqrt(_head_sum(kkv[u] * kkv[u]) + 1e-6) for u in units}
    kmod = {u: kr[u] * (1.0 + (a[u] - 1.0) * ka_ref[:, sl[u]]) for u in units}
    bvec = {u: kk[u] * a[u] for u in units}
    cl = {u: cw[u][C - 1:C, :] for u in units}
    e_out = {u: jnp.exp(-cw[u]) for u in units}
    e_end = {u: jnp.exp(cl[u] - cw[u]) for u in units}
    ar = {u: jnp.concatenate([-kk[u] * jnp.exp(cw[u] - lw[u]), r[u] * jnp.exp(cw[u])], axis=0).astype(bf16)
          for u in units}
    fb = {u: _dot_nt(ar[u], _block_diag((bvec[u] * e_out[u]).astype(bf16), G)) for u in units}
    fk = {u: _dot_nt(ar[u], _block_diag((kmod[u] * e_out[u]).astype(bf16), G)) for u in units}
    s = {(bb, g): state_ref[bb * n_groups + g] for bb, g in units}
    ars = {u: _dot_nt(ar[u], s[u]) for u in units}
    vbd = {u: _block_diag(v[u].astype(bf16), G) for u in units}
    tinv = dict(zip(units, _unit_lower_inverse([jnp.where(strict, -fb[u][:C], 0.0) for u in units], G)))
    z = {u: ars[u][:C] + _dot(jnp.where(strict, fk[u][:C], 0.0), vbd[u]) for u in units}
    uu = {u: _cat_matmul(tinv[u], z[u], G) for u in units}
    y = {u: ars[u][C:] + _cat_matmul(jnp.where(incl, fb[u][C:], 0.0), uu[u], G)
         + _dot(jnp.where(incl, fk[u][C:], 0.0), vbd[u]) for u in units}
    s_new = {u: jnp.where(same_head,
                          s[u] * jnp.exp(cl[u]) + _dot_tn(jnp.concatenate([uu[u], v[u]], axis=0),
                                                          jnp.concatenate([bvec[u] * e_end[u], kmod[u] * e_end[u]],
                                                                          axis=0)), 0.0) for u in units}
    for bb, g in units:
        u = (bb, g)
        state_ref[bb * n_groups + g] = s_new[u]
        mu = _head_sum(y[u]) * (1.0 / RWKV_HEAD)
        d = y[u] - mu
        var = _head_sum(d * d) * (1.0 / RWKV_HEAD)
        yn = d * lax.rsqrt(var + RWKV_LN_EPS) * lnw_ref[:, sl[u]] + lnb_ref[:, sl[u]]
        bonus = _head_sum(r[u] * kmod[u] * rk_ref[:, sl[u]]) * v[u]
        o_ref[bb, :, sl[u]] = ((yn + bonus) * gate_ref[bb, :, sl[u]]).astype(o_ref.dtype)


def _rwkv_chunk(r, k, v, lw, a, gate, k_k, k_a, r_k, ln_w, ln_b):
    B, S, R = r.shape
    nc = S // CHUNK
    W = RWKV_GROUP * RWKV_HEAD
    bb = _seqs_per_step(B)
    row = pl.BlockSpec((bb, CHUNK, R), lambda b, c: (b, c, 0))
    par = pl.BlockSpec((1, R), lambda b, c: (0, 0))
    params = [p.reshape(1, R) for p in (k_k, k_a, r_k, ln_w, ln_b)]
    return pl.pallas_call(
        _rwkv_chunk_kernel,
        grid=(B // bb, nc),
        in_specs=[row] * 6 + [par] * 5,
        out_specs=row,
        out_shape=jax.ShapeDtypeStruct((B, S, R), bf16),
        scratch_shapes=[pltpu.VMEM((bb * (R // W), W, W), f32)],
        compiler_params=_cparams(("parallel", "arbitrary")),
        name="rwkv_chunk",
    )(r, k, v, lw, a, gate, *params)


def _merge_kernel(yg_ref, yr_ref, h_ref, x_ref, gt_ref, nw_ref, wgate_ref, wug_ref, wur_ref, wo_ref, o_ref):
    D = x_ref.shape[1]
    sg = jax.nn.sigmoid(jnp.dot(h_ref[...], wgate_ref[...], preferred_element_type=f32))
    merged = (sg[:, :D] * jnp.dot(yg_ref[...], wug_ref[...], preferred_element_type=f32)
              + sg[:, D:] * jnp.dot(yr_ref[...], wur_ref[...], preferred_element_type=f32))
    out = _dot(merged, wo_ref[...])
    normed = out * lax.rsqrt(jnp.mean(out * out, axis=-1, keepdims=True) + NORM_EPS) * nw_ref[...]
    o_ref[...] = x_ref[...] + gt_ref[...] * normed


def _merge(yg, yr, h, wgate, x, gt, nw, wug, wur, wo, tm=256):
    B, S, D = x.shape
    full = lambda a: pl.BlockSpec(a.shape, lambda b, i: (0,) * a.ndim)
    row = lambda w: pl.BlockSpec((None, tm, w), lambda b, i: (b, i, 0))
    wug, wur, wo = wug.astype(bf16), wur.astype(bf16), wo.astype(bf16)
    nw = nw.reshape(1, D)
    return pl.pallas_call(
        _merge_kernel,
        grid=(B, S // tm),
        in_specs=[row(yg.shape[2]), row(yr.shape[2]), row(D), row(D),
                  pl.BlockSpec((None, 1, D), lambda b, i: (b, 0, 0)), full(nw), full(wgate), full(wug), full(wur),
                  full(wo)],
        out_specs=row(D),
        out_shape=jax.ShapeDtypeStruct((B, S, D), f32),
        compiler_params=_cparams(("parallel", "arbitrary")),
        name="merge",
    )(yg, yr, h, x, gt, nw, wgate, wug, wur, wo)


def _router_kernel(x_ref, sc_ref, sh_ref, nw_ref, whi_ref, wlo_ref, rb_ref, h_ref, sel_ref, prob_ref, cnt_ref,
                   carry_ref, *, n_experts):
    step = pl.program_id(0)

    @pl.when(step == 0)
    def _():
        carry_ref[...] = jnp.zeros_like(carry_ref)

    x = x_ref[...]
    tm = x.shape[0]
    ms = jnp.mean(x * x, axis=-1, keepdims=True)
    h = x * lax.rsqrt(ms + NORM_EPS) * (nw_ref[...] * (1.0 + sc_ref[...])) + sh_ref[...]
    h_ref[...] = _pack_bf16_pairs(h)
    hi = h.astype(bf16)
    lo = (h - hi.astype(f32)).astype(bf16)
    whi = whi_ref[...]
    logits = (jnp.dot(hi, whi, preferred_element_type=f32) + jnp.dot(lo, whi, preferred_element_type=f32)
              + jnp.dot(hi, wlo_ref[...], preferred_element_type=f32)) + rb_ref[...]
    lane = _iota(logits.shape, 1)
    neg = jnp.float32(-jnp.inf)
    cur = jnp.where(lane < n_experts, logits, neg)
    vals, idxs = [], []
    for _ in range(TOP_K):
        m = jnp.max(cur, axis=-1, keepdims=True)
        ix = jnp.min(jnp.where(cur == m, lane, LANES), axis=-1, keepdims=True)
        vals.append(m)
        idxs.append(ix)
        cur = jnp.where(lane == ix, neg, cur)
    es = [jnp.exp(vv - vals[0]) for vv in vals]
    denom = es[0] + es[1] + es[2] + es[3]
    onehots = [lane == ix for ix in idxs]
    msel = jnp.zeros(logits.shape, f32)
    for oh in onehots:
        msel = msel + jnp.where(oh, 1.0, 0.0)
    r = _iota((tm, tm), 0)
    c = _iota((tm, tm), 1)
    tri = jnp.where(c < r, 1.0, 0.0).astype(bf16)
    carry = carry_ref[0:1, :]
    pref = jnp.dot(tri, msel.astype(bf16), preferred_element_type=f32) + carry
    sel = jnp.zeros(logits.shape, i32)
    prob = jnp.zeros(logits.shape, f32)
    for kk in range(TOP_K):
        rank = jnp.sum(jnp.where(onehots[kk], pref, 0.0), axis=-1, keepdims=True).astype(i32)
        sel = jnp.where(lane == kk, idxs[kk], sel)
        sel = jnp.where(lane == TOP_K + kk, rank, sel)
        prob = jnp.where(lane == kk, es[kk] / denom, prob)
    sel_ref[...] = sel
    prob_ref[...] = prob
    new_carry = carry + jnp.sum(msel, axis=0, keepdims=True)
    carry_ref[...] = jnp.broadcast_to(new_carry, carry_ref.shape)
    cnt_ref[...] = jnp.broadcast_to(new_carry, cnt_ref.shape)


def _router(x1, sc, sh, nw, rw, rb, tm=256):
    B, S, D = x1.shape
    E = rw.shape[1]
    rwp = _pad_to(rw, 1, LANES)
    whi = rwp.astype(bf16)
    wlo = (rwp - whi.astype(f32)).astype(bf16)
    rbp = _pad_to(rb.reshape(1, E), 1, LANES)
    spt = S // tm
    row = lambda w: pl.BlockSpec((None, tm, w), lambda i: (i // spt, i % spt, 0))
    per_b = pl.BlockSpec((None, 1, D), lambda i: (i // spt, 0, 0))
    full = lambda a: pl.BlockSpec(a.shape, lambda i: (0,) * a.ndim)
    nw = nw.reshape(1, D)
    return pl.pallas_call(
        functools.partial(_router_kernel, n_experts=E),
        grid=(B * spt,),
        in_specs=[row(D), per_b, per_b, full(nw), full(whi), full(wlo), full(rbp)],
        out_specs=[row(D // 2), row(LANES), row(LANES), pl.BlockSpec((SUBLANES, LANES), lambda i: (0, 0))],
        out_shape=[jax.ShapeDtypeStruct((B, S, D // 2), jnp.uint32), jax.ShapeDtypeStruct((B, S, LANES), i32),
                   jax.ShapeDtypeStruct((B, S, LANES), f32), jax.ShapeDtypeStruct((SUBLANES, LANES), f32)],
        scratch_shapes=[pltpu.VMEM((SUBLANES, LANES), f32)],
        compiler_params=_cparams(("arbitrary",)),
        name="router",
    )(x1, sc, sh, nw, whi, wlo, rbp)


def _dispatch_kernel(pos_ref, h_ref, init_ref, xs_ref, sem):
    del init_ref
    tm = h_ref.shape[0]

    def row_copy(r, kk):
        p = pos_ref[0, r * TOP_K + kk]
        return pltpu.make_async_copy(h_ref.at[pl.ds(r, 1), :], xs_ref.at[pl.ds(p, 1), :], sem)

    def issue(r, carry):
        for kk in range(TOP_K):
            row_copy(r, kk).start(priority=kk % 2)
        return carry

    def drain(r, carry):
        for kk in range(TOP_K):
            row_copy(r, kk).wait()
        return carry

    lax.fori_loop(0, tm, issue, 0)
    lax.fori_loop(0, tm, drain, 0)


def _dispatch(h2, pos, n_rows, tm=256):
    T, D = h2.shape
    pos3 = pos.reshape(T // tm, 1, tm * TOP_K)
    init = jnp.zeros((n_rows, D), h2.dtype)
    return pl.pallas_call(
        _dispatch_kernel,
        grid=(T // tm,),
        in_specs=[pl.BlockSpec((None, 1, tm * TOP_K), lambda i: (i, 0, 0), memory_space=pltpu.SMEM),
                  pl.BlockSpec((tm, D), lambda i: (i, 0)),
                  pl.BlockSpec(memory_space=pl.ANY)],
        out_specs=pl.BlockSpec(memory_space=pl.ANY),
        out_shape=jax.ShapeDtypeStruct((n_rows, D), h2.dtype),
        scratch_shapes=[pltpu.SemaphoreType.DMA],
        input_output_aliases={2: 0},
        compiler_params=_cparams(("arbitrary",)),
        name="dispatch",
    )(pos3, h2, init)


def _expert_kernel(be_ref, nv_ref, nu_ref, x_ref, wg_ref, wu_ref, bg_ref, bu_ref, wd_ref, bd_ref, o_ref, xb_ref,
                   acc_ref):
    del be_ref, nu_ref
    nb = pl.program_id(0)
    f = pl.program_id(1)
    n_valid = nv_ref[nb]

    @pl.when((n_valid > 0) & (f == 0))
    def _():
        xb_ref[...] = _unpack_bf16_pairs(x_ref[...])
        acc_ref[...] = jnp.broadcast_to(bd_ref[...], acc_ref.shape)

    @pl.when((n_valid == 0) & (f == 0))
    def _():
        o_ref[...] = jnp.zeros_like(o_ref)

    n_sub = (n_valid + EXPERT_SUB_ROWS - 1) // EXPERT_SUB_ROWS
    for ns in range(1, EXPERT_ROWS // EXPERT_SUB_ROWS + 1):
        rows = slice(0, ns * EXPERT_SUB_ROWS)

        @pl.when(n_sub == ns)
        def _():
            x = xb_ref[rows, :]
            g = jnp.dot(x, wg_ref[...].astype(bf16), preferred_element_type=f32) + bg_ref[...]
            u = jnp.dot(x, wu_ref[...].astype(bf16), preferred_element_type=f32) + bu_ref[...]
            gl = jnp.minimum(g, SWIGLU_LIMIT)
            up = jnp.clip(u, -SWIGLU_LIMIT, SWIGLU_LIMIT)
            act = gl * jax.nn.sigmoid(SWIGLU_ALPHA * gl) * (up + 1.0)
            acc_ref[rows, :] += jnp.dot(act.astype(bf16), wd_ref[...].astype(bf16), preferred_element_type=f32)

    @pl.when((n_valid > 0) & (f == pl.num_programs(1) - 1))
    def _():
        o_ref[...] = _pack_bf16_pairs(acc_ref[...])


def _experts(xs, block_e, n_valid, n_used, w_gu, b_gu, w_down, b_down):
    P, Dh = xs.shape
    D = 2 * Dh
    E, _, F2 = w_gu.shape
    F = F2 // 2
    tf = EXPERT_FF_TILE
    nf = F // tf
    NB = P // EXPERT_ROWS
    b_gu3 = b_gu.reshape(E, 1, F2)
    b_dn3 = b_down.reshape(E, 1, D)

    def blk(nb, nu):
        return jnp.minimum(nb, nu[0] - 1)

    def fidx(nb, f, nu):
        return jnp.where(nb < nu[0], f, nf - 1)

    grid_spec = pltpu.PrefetchScalarGridSpec(
        num_scalar_prefetch=3,
        grid=(NB, nf),
        in_specs=[
            pl.BlockSpec((EXPERT_ROWS, Dh), lambda nb, f, be, nv, nu: (blk(nb, nu), 0)),
            pl.BlockSpec((None, D, tf), lambda nb, f, be, nv, nu: (be[blk(nb, nu)], 0, fidx(nb, f, nu))),
            pl.BlockSpec((None, D, tf), lambda nb, f, be, nv, nu: (be[blk(nb, nu)], 0, nf + fidx(nb, f, nu))),
            pl.BlockSpec((None, 1, tf), lambda nb, f, be, nv, nu: (be[blk(nb, nu)], 0, fidx(nb, f, nu))),
            pl.BlockSpec((None, 1, tf), lambda nb, f, be, nv, nu: (be[blk(nb, nu)], 0, nf + fidx(nb, f, nu))),
            pl.BlockSpec((None, tf, D), lambda nb, f, be, nv, nu: (be[blk(nb, nu)], fidx(nb, f, nu), 0)),
            pl.BlockSpec((None, 1, D), lambda nb, f, be, nv, nu: (be[blk(nb, nu)], 0, 0)),
        ],
        out_specs=pl.BlockSpec((EXPERT_ROWS, Dh), lambda nb, f, be, nv, nu: (nb, 0)),
        scratch_shapes=[pltpu.VMEM((EXPERT_ROWS, D), bf16), pltpu.VMEM((EXPERT_ROWS, D), f32)],
    )
    return pl.pallas_call(
        _expert_kernel,
        grid_spec=grid_spec,
        out_shape=jax.ShapeDtypeStruct((P, Dh), jnp.uint32),
        compiler_params=_cparams(("arbitrary", "arbitrary")),
        name="experts",
    )(block_e, n_valid, n_used, xs, w_gu, w_gu, b_gu3, b_gu3, w_down, b_dn3)


def _combine_kernel(pos_ref, yb_ref, prob_ref, x_ref, gt_ref, nw_ref, o_ref, buf_ref, sem):
    tm = x_ref.shape[0]

    def row_copy(r, kk):
        p = pos_ref[0, r * TOP_K + kk]
        return pltpu.make_async_copy(yb_ref.at[pl.ds(p, 1), :], buf_ref.at[kk, pl.ds(r, 1), :], sem)

    def issue(r, carry):
        for kk in range(TOP_K):
            row_copy(r, kk).start(priority=kk % 2)
        return carry

    def drain(r, carry):
        for kk in range(TOP_K):
            row_copy(r, kk).wait()
        return carry

    lax.fori_loop(0, tm, issue, 0)
    lax.fori_loop(0, tm, drain, 0)
    prob = prob_ref[...]
    y_lo = None
    for kk in range(TOP_K):
        u = buf_ref[kk]
        lo = lax.bitcast_convert_type(u << 16, f32) * prob[:, kk:kk + 1]
        hi = lax.bitcast_convert_type(u & jnp.uint32(0xFFFF0000), f32) * prob[:, kk:kk + 1]
        y_lo, y_hi = (lo, hi) if y_lo is None else (y_lo + lo, y_hi + hi)
    y = jnp.concatenate([y_lo, y_hi], axis=1)
    normed = y * lax.rsqrt(jnp.mean(y * y, axis=-1, keepdims=True) + NORM_EPS) * nw_ref[...]
    o_ref[...] = x_ref[...] + gt_ref[...] * normed


def _combine(yb, pos, prob, x1, gt, nw, tm=256):
    B, S, D = x1.shape
    T = B * S
    spt = S // tm
    pos3 = pos.reshape(T // tm, 1, tm * TOP_K)
    row = lambda w: pl.BlockSpec((None, tm, w), lambda i: (i // spt, i % spt, 0))
    nw = nw.reshape(1, D)
    return pl.pallas_call(
        _combine_kernel,
        grid=(T // tm,),
        in_specs=[pl.BlockSpec((None, 1, tm * TOP_K), lambda i: (i, 0, 0), memory_space=pltpu.SMEM),
                  pl.BlockSpec(memory_space=pl.ANY),
                  row(LANES), row(D),
                  pl.BlockSpec((None, 1, D), lambda i: (i // spt, 0, 0)),
                  pl.BlockSpec(nw.shape, lambda i: (0, 0))],
        out_specs=row(D),
        out_shape=jax.ShapeDtypeStruct((B, S, D), f32),
        scratch_shapes=[pltpu.VMEM((TOP_K, tm, D // 2), jnp.uint32), pltpu.SemaphoreType.DMA],
        compiler_params=_cparams(("arbitrary",)),
        name="combine",
    )(pos3, yb, prob, x1, gt, nw)


def _mixer(x, sc1, sh1, gt1, norm_pre, norm_post, w_in, gdn_conv, gdn_A_log, gdn_dt_bias, gdn_norm, rwkv_mix,
           rwkv_w0, rwkv_w1, rwkv_w2, rwkv_a0, rwkv_a1, rwkv_a2, rwkv_g1, rwkv_g2, rwkv_k_k, rwkv_k_a, rwkv_r_k,
           rwkv_ln_w, rwkv_ln_b, w_up_gdn, w_up_rwkv, w_out):
    B, S, D = x.shape
    T = B * S
    R = rwkv_w2.shape[1]
    col_a = COL_B + GDN_V_HEADS
    col_gate = col_a + GDN_V_HEADS
    col_r = col_gate + 2 * D
    h, xr, xk, xv, lw, a_sig, gate = _premix(x, sc1, sh1, norm_pre, rwkv_mix, rwkv_w1, rwkv_w2, rwkv_w0, rwkv_a1,
                                             rwkv_a2, rwkv_a0, rwkv_g1, rwkv_g2)
    wb = w_in.astype(bf16)
    h2d = h.reshape(T, D)
    qkvz = _mm(h2d, wb[:, :COL_B]).reshape(B, S, COL_B)
    ba = _mm(h2d, _pad_to(wb[:, COL_B:col_gate], 1, LANES)).reshape(B, S, LANES)
    r = _mm(xr.reshape(T, D), wb[:, col_r:col_r + R]).reshape(B, S, R)
    k_r = _mm(xk.reshape(T, D), wb[:, col_r + R:col_r + 2 * R]).reshape(B, S, R)
    v_r = _mm(xv.reshape(T, D), wb[:, col_r + 2 * R:col_r + 3 * R]).reshape(B, S, R)

    q, k, v, bg = _gdn_prep(qkvz, ba, gdn_conv, gdn_A_log, gdn_dt_bias)
    nc = S // CHUNK
    grow = bg[:, :, GDN_V_HEADS:2 * GDN_V_HEADS].reshape(B, nc, CHUNK, GDN_V_HEADS)
    grow = grow.transpose(0, 1, 3, 2).reshape(B, nc, 1, GDN_V_HEADS * CHUNK)
    y_gdn = _gdn_chunk(q, k, v, qkvz, bg, grow, gdn_norm)
    y_rwkv = _rwkv_chunk(r, k_r, v_r, lw, a_sig, gate, rwkv_k_k, rwkv_k_a, rwkv_r_k.reshape(-1), rwkv_ln_w,
                         rwkv_ln_b)
    return _merge(y_gdn, y_rwkv, h, wb[:, col_gate:col_r], x, gt1, norm_post, w_up_gdn, w_up_rwkv, w_out)


def _moe(x1, sc2, sh2, gt2, norm_pre, norm_post, router_w, router_b, w_gu, b_gu, w_down, b_down):
    B, S, D = x1.shape
    T = B * S
    E = router_w.shape[1]
    h2, sel, prob, cnt = _router(x1, sc2, sh2, norm_pre, router_w, router_b)
    sel = sel.reshape(T, LANES)
    e_idx = sel[:, :TOP_K]
    rank = sel[:, TOP_K:2 * TOP_K]
    counts = cnt[0, :E].astype(i32)
    padded = (counts + EXPERT_ROWS - 1) // EXPERT_ROWS * EXPERT_ROWS
    pad_end = jnp.cumsum(padded)
    pad_start = pad_end - padded
    pos = (pad_start[e_idx] + rank).astype(i32)
    P = T * TOP_K + E * EXPERT_ROWS
    NB = P // EXPERT_ROWS
    block_start = jnp.arange(NB, dtype=i32) * EXPERT_ROWS
    block_e = jnp.minimum(jnp.sum(pad_end[None, :] <= block_start[:, None], axis=1), E - 1).astype(i32)
    n_valid = jnp.clip(pad_start[block_e] + counts[block_e] - block_start, 0, EXPERT_ROWS).astype(i32)
    n_used = jnp.maximum(pad_end[-1:] // EXPERT_ROWS, 1).astype(i32)
    xs = _dispatch(h2.reshape(T, D // 2), pos, P)
    yb = _experts(xs, block_e, n_valid, n_used, w_gu, b_gu, w_down, b_down)
    return _combine(yb, pos, prob, x1, gt2, norm_post)


def kernel(x, c, ada_w, ada_b, norm_mix_pre, norm_mix_post, norm_ffn_pre, norm_ffn_post, w_in, gdn_conv, gdn_A_log,
           gdn_dt_bias, gdn_norm, rwkv_mix, rwkv_w0, rwkv_w1, rwkv_w2, rwkv_a0, rwkv_a1, rwkv_a2, rwkv_g1, rwkv_g2,
           rwkv_k_k, rwkv_k_a, rwkv_r_k, rwkv_ln_w, rwkv_ln_b, w_up_gdn, w_up_rwkv, w_out, router_w, router_b,
           exp_w_gu, exp_b_gu, exp_w_down, exp_b_down):
    B, S, D = x.shape
    for l in range(ada_w.shape[0]):
        mod = _ada(c, ada_w[l], ada_b[l])
        sh1, sc1, gt1, sh2, sc2, gt2 = [m.reshape(B, 1, D) for m in jnp.split(mod, 6, axis=-1)]
        x = _mixer(x, sc1, sh1, gt1, norm_mix_pre[l], norm_mix_post[l], w_in[l], gdn_conv[l], gdn_A_log[l],
                   gdn_dt_bias[l], gdn_norm[l], rwkv_mix[l], rwkv_w0[l], rwkv_w1[l], rwkv_w2[l], rwkv_a0[l],
                   rwkv_a1[l], rwkv_a2[l], rwkv_g1[l], rwkv_g2[l], rwkv_k_k[l], rwkv_k_a[l], rwkv_r_k[l],
                   rwkv_ln_w[l], rwkv_ln_b[l], w_up_gdn[l], w_up_rwkv[l], w_out[l])
        x = _moe(x, sc2, sh2, gt2, norm_ffn_pre[l], norm_ffn_post[l], router_w[l], router_b[l], exp_w_gu[l],
                 exp_b_gu[l], exp_w_down[l], exp_b_down[l])
    return x
```
